```python
import math
import jax, jax.numpy as jnp
from jax import lax
import numpy as np

D_MODEL = 1024
BATCH = 8
SEQ = 2048
DEPTH = 2
DEC_BATCH = 16
DEC_SEQ = 4096
PAST_LEN = 128

HEAD_DIM = 64
MLA_HEADS = 8
MLA_Q_LORA = 384
MLA_KV_LORA = 256
MLA_NOPE = 64
MLA_ROPE = 32
MLA_V = 64
ROPE_THETA = 10000.0
MLA_Q_BLOCK = 128
DIL_PAIRS = ((128, 1), (512, 4), (2048, 16))
N_DIL_GROUPS = 3
DIL_HEADS = 4
SWA_Q_HEADS = 4
SWA_KV_HEADS = 2
SWA_HALF_WINDOW = 128
N_ALIBI_HEADS = SWA_Q_HEADS + N_DIL_GROUPS * DIL_HEADS
MOE_GROUPS = 4
MOE_EXPERTS_PER_GROUP = 8
MOE_EXPERTS = MOE_GROUPS * MOE_EXPERTS_PER_GROUP
MOE_TOP_K = 2
MOE_HIDDEN = 256
RMS_EPS = 1e-6
NEG_INF = -1e30

MIX_WIDTH = MLA_HEADS * MLA_V + DIL_HEADS * HEAD_DIM + SWA_Q_HEADS * HEAD_DIM
A_COLS = MLA_Q_LORA + MLA_KV_LORA + MLA_ROPE
B_COLS = N_DIL_GROUPS * 3 * DIL_HEADS * HEAD_DIM
C_COLS = (SWA_Q_HEADS + 2 * SWA_KV_HEADS) * HEAD_DIM
IN_COLS = A_COLS + B_COLS + C_COLS

kernel_name = 'hybrid_bidir_mla_dilated_swa_hmoe'


def alibi_slopes():
    return 2.0 ** (-8.0 * jnp.arange(1, N_ALIBI_HEADS + 1, dtype=jnp.float32) / N_ALIBI_HEADS)


def rms_norm(x, g):
    xf = x.astype(jnp.float32)
    y = xf * lax.rsqrt(jnp.mean(xf * xf, axis=-1, keepdims=True) + RMS_EPS)
    return (y * g.astype(jnp.float32)).astype(x.dtype)


def apply_rope(x, cos, sin):
    x1, x2 = jnp.split(x.astype(jnp.float32), 2, axis=-1)
    return jnp.concatenate([x1 * cos - x2 * sin, x2 * cos + x1 * sin], axis=-1).astype(x.dtype)


def banded_attention(q, k, v, half_w, slopes, dist_scale, sink):
    N, L, Hkv, G, D = q.shape
    blk = half_w
    nb = -(-L // blk)
    Lp = nb * blk
    pad = Lp - L
    qb = jnp.pad(q, ((0, 0), (0, pad), (0, 0), (0, 0), (0, 0))).reshape(N, nb, blk, Hkv, G, D)
    def key_blocks(t):
        tp = jnp.pad(t, ((0, 0), (blk, blk + pad), (0, 0), (0, 0))).reshape(N, nb + 2, blk, Hkv, D)
        return jnp.concatenate([tp[:, :-2], tp[:, 1:-1], tp[:, 2:]], axis=2)
    kb, vb = key_blocks(k), key_blocks(v)
    s = jnp.einsum('nbqhgd,nbkhd->nbhgqk', qb, kb).astype(jnp.float32) * (D ** -0.5)
    rel = jnp.arange(3 * blk)[None, :] - blk - jnp.arange(blk)[:, None]
    kpos = jnp.arange(nb)[:, None, None] * blk - blk + jnp.arange(3 * blk)[None, None, :]
    valid = (jnp.abs(rel) <= half_w)[None] & (kpos >= 0) & (kpos < L)
    dist = jnp.abs(rel).astype(jnp.float32) * dist_scale
    s = s - slopes.astype(jnp.float32)[:, :, None, None] * dist
    s = jnp.where(valid[None, :, None, None], s, NEG_INF)
    m = jnp.max(s, axis=-1)
    if sink is not None:
        m = jnp.maximum(m, sink[:, :, None])
    p = jnp.exp(s - m[..., None])
    den = jnp.sum(p, axis=-1)
    if sink is not None:
        den = den + jnp.exp(sink[:, :, None] - m)
    out = jnp.einsum('nbhgqk,nbkhd->nbqhgd', (p / den[..., None]).astype(v.dtype), vb)
    out = out.reshape(N, Lp, Hkv, G, D)[:, :L]
    lse = (m + jnp.log(den)).transpose(0, 1, 4, 2, 3).reshape(N, Lp, Hkv, G)[:, :L]
    return out, lse


def mla_attention(c_q, c_kv, k_rope, q_norm_g, w_uq, kv_norm_g, w_ukv):
    B, S, _ = c_q.shape
    q = (rms_norm(c_q, q_norm_g) @ w_uq).reshape(B, S, MLA_HEADS, MLA_NOPE + MLA_ROPE)
    kv = (rms_norm(c_kv, kv_norm_g) @ w_ukv).reshape(B, S, MLA_HEADS, MLA_NOPE + MLA_V)
    q_nope, q_pe = q[..., :MLA_NOPE], q[..., MLA_NOPE:]
    k_nope, v = kv[..., :MLA_NOPE], kv[..., MLA_NOPE:]
    pos = jnp.arange(S, dtype=jnp.float32)
    freqs = ROPE_THETA ** (-jnp.arange(0, MLA_ROPE, 2, dtype=jnp.float32) / MLA_ROPE)
    ang = pos[:, None] * freqs[None, :]
    cos, sin = jnp.cos(ang), jnp.sin(ang)
    q_pe = apply_rope(q_pe, cos[:, None, :], sin[:, None, :])
    k_pe = apply_rope(k_rope, cos, sin)
    scale = (MLA_NOPE + MLA_ROPE) ** -0.5
    nq = S // MLA_Q_BLOCK
    def to_blocks(t):
        return t.reshape(B, nq, MLA_Q_BLOCK, MLA_HEADS, t.shape[-1]).swapaxes(0, 1)
    def attend(qs):
        qn, qp = qs
        s = (jnp.einsum('bqhd,bkhd->bhqk', qn, k_nope)
             + jnp.einsum('bqhr,bkr->bhqk', qp, k_pe)).astype(jnp.float32) * scale
        p = jax.nn.softmax(s, axis=-1)
        return jnp.einsum('bhqk,bkhd->bqhd', p.astype(v.dtype), v)
    o = lax.map(attend, (to_blocks(q_nope), to_blocks(q_pe)))
    return o.swapaxes(0, 1).reshape(B, S, MLA_HEADS * MLA_V)


def dilated_attention(qkv, slopes_b):
    B, S = qkv.shape[0], qkv.shape[1]
    outs, lses = [], []
    for i, (window, r) in enumerate(DIL_PAIRS):
        Ls = S // r
        def to_sub(t):
            return t.reshape(B, Ls, r, DIL_HEADS, HEAD_DIM).swapaxes(1, 2).reshape(B * r, Ls, DIL_HEADS, HEAD_DIM)
        q, k, v = qkv[:, :, i, 0], qkv[:, :, i, 1], qkv[:, :, i, 2]
        o, lse = banded_attention(to_sub(q)[:, :, :, None, :], to_sub(k), to_sub(v),
                                  window // (2 * r), slopes_b[i][:, None], r, None)
        outs.append(o[:, :, :, 0].reshape(B, r, Ls, DIL_HEADS, HEAD_DIM).swapaxes(1, 2).reshape(B, S, DIL_HEADS, HEAD_DIM))
        lses.append(lse[:, :, :, 0].reshape(B, r, Ls, DIL_HEADS).swapaxes(1, 2).reshape(B, S, DIL_HEADS))
    wts = jax.nn.softmax(jnp.stack(lses), axis=0)
    out = jnp.sum(wts[..., None] * jnp.stack(outs).astype(jnp.float32), axis=0)
    return out.astype(qkv.dtype).reshape(B, S, DIL_HEADS * HEAD_DIM)


def window_gqa(pc, slopes_c, sink):
    B, S, _ = pc.shape
    nq = SWA_Q_HEADS * HEAD_DIM
    nk = SWA_KV_HEADS * HEAD_DIM
    G = SWA_Q_HEADS // SWA_KV_HEADS
    q = pc[..., :nq].reshape(B, S, SWA_KV_HEADS, G, HEAD_DIM)
    k = pc[..., nq:nq + nk].reshape(B, S, SWA_KV_HEADS, HEAD_DIM)
    v = pc[..., nq + nk:].reshape(B, S, SWA_KV_HEADS, HEAD_DIM)
    o, _ = banded_attention(q, k, v, SWA_HALF_WINDOW, slopes_c, 1,
                            sink.astype(jnp.float32).reshape(SWA_KV_HEADS, G))
    return o.reshape(B, S, nq)


def hier_moe(h, w_rg, w_re, w_g, w_u, w_d):
    B, S, D = h.shape
    pg = jax.nn.softmax((h @ w_rg).astype(jnp.float32), axis=-1)
    gsel = jnp.argmax(pg, axis=-1)
    gprob = jnp.max(pg, axis=-1)
    le = (h @ w_re).astype(jnp.float32).reshape(B, S, MOE_GROUPS, MOE_EXPERTS_PER_GROUP)
    le_sel = jnp.take_along_axis(le, gsel[..., None, None], axis=2)[..., 0, :]
    top_l, top_i = lax.top_k(le_sel, MOE_TOP_K)
    gate = jax.nn.softmax(top_l, axis=-1) * gprob[..., None]
    eid = gsel[..., None] * MOE_EXPERTS_PER_GROUP + top_i
    combine = jnp.sum(jax.nn.one_hot(eid, MOE_EXPERTS, dtype=jnp.float32) * gate[..., None], axis=-2)
    def expert_mix(args):
        hb, cb = args
        a = jnp.einsum('sd,edf->sef', hb, w_g)
        u = jnp.einsum('sd,edf->sef', hb, w_u)
        z = (jax.nn.silu(a.astype(jnp.float32)) * u.astype(jnp.float32)) * cb[..., None]
        return jnp.einsum('sef,efd->sd', z.astype(hb.dtype), w_d)
    return lax.map(expert_mix, (h, combine)).astype(h.dtype)


def trunk(x, norm1_g, w_in, mla_q_norm_g, mla_w_uq, mla_kv_norm_g, mla_w_ukv, swa_sink, w_out,
          norm2_g, w_router_group, w_router_expert, w_gate, w_up, w_down, final_norm_g):
    B, S, _ = x.shape
    slopes = alibi_slopes()
    slopes_c = slopes[:SWA_Q_HEADS].reshape(SWA_KV_HEADS, SWA_Q_HEADS // SWA_KV_HEADS)
    slopes_b = slopes[SWA_Q_HEADS:].reshape(N_DIL_GROUPS, DIL_HEADS)
    for l in range(DEPTH):
        h = rms_norm(x, norm1_g[l])
        proj = h @ w_in[l]
        pa = proj[..., :A_COLS]
        pb = proj[..., A_COLS:A_COLS + B_COLS]
        pc = proj[..., A_COLS + B_COLS:]
        ya = mla_attention(pa[..., :MLA_Q_LORA], pa[..., MLA_Q_LORA:MLA_Q_LORA + MLA_KV_LORA],
                           pa[..., MLA_Q_LORA + MLA_KV_LORA:], mla_q_norm_g[l], mla_w_uq[l],
                           mla_kv_norm_g[l], mla_w_ukv[l])
        yb = dilated_attention(pb.reshape(B, S, N_DIL_GROUPS, 3, DIL_HEADS, HEAD_DIM), slopes_b)
        yc = window_gqa(pc, slopes_c, swa_sink[l])
        x = x + jnp.concatenate([ya, yb, yc], axis=-1) @ w_out[l]
        x = x + hier_moe(rms_norm(x, norm2_g[l]), w_router_group[l], w_router_expert[l],
                         w_gate[l], w_up[l], w_down[l])
    return rms_norm(x, final_norm_g)


def setup_inputs(seed: int = 0) -> dict:
    key = jax.random.key(seed)
    ks = jax.random.split(key, 20)
    f32 = jnp.float32
    nrm = lambda k, shape: jax.random.normal(k, shape, dtype=f32)
    return {
        'x_prompt': nrm(ks[0], (BATCH, SEQ, D_MODEL)),
        'x_sample': nrm(ks[1], (DEC_BATCH, DEC_SEQ, D_MODEL)),
        'norm1_g': 1.0 + 0.01 * nrm(ks[2], (DEPTH, D_MODEL)),
        'w_in': nrm(ks[3], (DEPTH, D_MODEL, IN_COLS)) * D_MODEL ** -0.5,
        'mla_q_norm_g': 1.0 + 0.01 * nrm(ks[4], (DEPTH, MLA_Q_LORA)),
        'mla_w_uq': nrm(ks[5], (DEPTH, MLA_Q_LORA, MLA_HEADS * (MLA_NOPE + MLA_ROPE))) * MLA_Q_LORA ** -0.5,
        'mla_kv_norm_g': 1.0 + 0.01 * nrm(ks[6], (DEPTH, MLA_KV_LORA)),
        'mla_w_ukv': nrm(ks[7], (DEPTH, MLA_KV_LORA, MLA_HEADS * (MLA_NOPE + MLA_V))) * MLA_KV_LORA ** -0.5,
        'swa_sink': 0.5 * nrm(ks[8], (DEPTH, SWA_Q_HEADS)),
        'w_out': nrm(ks[9], (DEPTH, MIX_WIDTH, D_MODEL)) * MIX_WIDTH ** -0.5,
        'norm2_g': 1.0 + 0.01 * nrm(ks[10], (DEPTH, D_MODEL)),
        'w_router_group': nrm(ks[11], (DEPTH, D_MODEL, MOE_GROUPS)) * D_MODEL ** -0.5,
        'w_router_expert': nrm(ks[12], (DEPTH, D_MODEL, MOE_EXPERTS)) * D_MODEL ** -0.5,
        'w_gate': nrm(ks[13], (DEPTH, MOE_EXPERTS, D_MODEL, MOE_HIDDEN)) * D_MODEL ** -0.5,
        'w_up': nrm(ks[14], (DEPTH, MOE_EXPERTS, D_MODEL, MOE_HIDDEN)) * D_MODEL ** -0.5,
        'w_down': nrm(ks[15], (DEPTH, MOE_EXPERTS, MOE_HIDDEN, D_MODEL)) * MOE_HIDDEN ** -0.5,
        'final_norm_g': 1.0 + 0.01 * nrm(ks[16], (D_MODEL,)),
    }


def reference(x_prompt, x_sample, norm1_g, w_in, mla_q_norm_g, mla_w_uq, mla_kv_norm_g, mla_w_ukv,
              swa_sink, w_out, norm2_g, w_router_group, w_router_expert, w_gate, w_up, w_down,
              final_norm_g):
    y_prompt = trunk(x_prompt, norm1_g, w_in, mla_q_norm_g, mla_w_uq, mla_kv_norm_g, mla_w_ukv,
                     swa_sink, w_out, norm2_g, w_router_group, w_router_expert, w_gate, w_up,
                     w_down, final_norm_g)
    y_sample = trunk(x_sample, norm1_g, w_in, mla_q_norm_g, mla_w_uq, mla_kv_norm_g, mla_w_ukv,
                     swa_sink, w_out, norm2_g, w_router_group, w_router_expert, w_gate, w_up,
                     w_down, final_norm_g)
    return (y_prompt, y_sample)
```

```python
import functools

import numpy as np
import jax
import jax.numpy as jnp
from jax import lax
from jax.experimental import pallas as pl
from jax.experimental.pallas import tpu as pltpu

F32 = jnp.float32
BF16 = jnp.bfloat16

D_MODEL = 1024
DEPTH = 2
HEAD_DIM = 64
MLA_HEADS = 8
MLA_Q_LORA = 384
MLA_KV_LORA = 256
MLA_NOPE = 64
MLA_ROPE = 32
MLA_V = 64
ROPE_THETA = 10000.0
DIL_PAIRS = ((128, 1), (512, 4), (2048, 16))
DIL_HEADS = 4
SWA_Q_HEADS = 4
SWA_KV_HEADS = 2
SWA_HALF_WINDOW = 128
N_ALIBI_HEADS = 16
MOE_GROUPS = 4
MOE_EPG = 8
MOE_EXPERTS = 32
MOE_HIDDEN = 256
RMS_EPS = 1e-6
NEG_INF = -1e30

LANES = 128
MLA_HEAD_PAD = 128
A_COLS = MLA_Q_LORA + MLA_KV_LORA + MLA_ROPE
B_GROUP_COLS = 3 * DIL_HEADS * HEAD_DIM
B_COLS = 3 * B_GROUP_COLS
OFF_CQ = 0
OFF_CKV = MLA_Q_LORA
OFF_KRA = OFF_CKV + MLA_KV_LORA
OFF_KRB = OFF_KRA + LANES
OFF_G1 = OFF_KRB + LANES
OFF_G2 = OFF_G1 + B_GROUP_COLS
OFF_G3 = OFF_G2 + B_GROUP_COLS
OFF_GC = OFF_G3 + B_GROUP_COLS
BIG_COLS = OFF_GC + B_GROUP_COLS

N_PAIR_CLASSES = MOE_GROUPS * (MOE_EPG * (MOE_EPG - 1) // 2)
MOE_TILE = 256
TOKEN_TILE = 256
VMEM_LIMIT = 56 * 1024 * 1024


def _alibi_slopes():
    return 2.0 ** (-8.0 * np.arange(1, N_ALIBI_HEADS + 1, dtype=np.float64) / N_ALIBI_HEADS)


def _rms(x, g):
    return x * lax.rsqrt(jnp.mean(x * x, axis=-1, keepdims=True) + RMS_EPS) * g


def _dot(a, b):
    return jnp.dot(a, b, preferred_element_type=F32)


def _dot_nt(a, b):
    return lax.dot_general(a, b, (((1,), (1,)), ((), ())), preferred_element_type=F32)


def _const_spec(shape):
    nd = len(shape)
    return pl.BlockSpec(shape, lambda *_: (0,) * nd, pipeline_mode=pl.Buffered(1))


def _in_proj_kernel(*refs, has_y, tm):
    if has_y:
        x_ref, y_ref = refs[:2]
        refs = refs[2:]
    else:
        x_ref = refs[0]
        refs = refs[1:]
    (g1_ref, wbig_ref, qg_ref, wq_ref, kvg_ref, wkv_ref, cos_ref, sin_ref) = refs[:8]
    refs = refs[8:]
    if has_y:
        xo_ref = refs[0]
        refs = refs[1:]
    q_ref, k_ref, v_ref, b1_ref, b2_ref, b3_ref, c_ref, scr_ref = refs

    x = x_ref[...]
    if has_y:
        x = x + y_ref[...].astype(F32)
        xo_ref[...] = x
    h = _rms(x, g1_ref[...]).astype(BF16)
    proj = _dot(h, wbig_ref[...])

    cos = cos_ref[...]
    sin = sin_ref[...]
    cos8 = jnp.concatenate([cos] * MLA_HEADS, axis=1)
    sin8 = jnp.concatenate([sin] * MLA_HEADS, axis=1)
    hw = MLA_HEADS * MLA_HEAD_PAD

    cqn = _rms(proj[:, OFF_CQ:OFF_CKV], qg_ref[...]).astype(BF16)
    qa = _dot(cqn, wq_ref[...])
    scale = float((MLA_NOPE + MLA_ROPE) ** -0.5)
    q = (qa[:, :hw] * cos8 + qa[:, hw:] * sin8) * scale
    q_ref[...] = q.astype(BF16)

    ckvn = _rms(proj[:, OFF_CKV:OFF_KRA], kvg_ref[...]).astype(BF16)
    kva = _dot(ckvn, wkv_ref[...])
    kpe = proj[:, OFF_KRA:OFF_KRB] * cos + proj[:, OFF_KRB:OFF_G1] * sin
    k = kva[:, :hw] + jnp.concatenate([kpe] * MLA_HEADS, axis=1)
    k_ref[...] = k.astype(BF16)
    v_ref[...] = kva[:, hw:].astype(BF16)

    b1_ref[...] = proj[:, OFF_G1:OFF_G2].astype(BF16)
    c_ref[...] = proj[:, OFF_GC:BIG_COLS].astype(BF16)

    nslab = B_GROUP_COLS // LANES
    for off, r, out_ref in ((OFF_G2, DIL_PAIRS[1][1], b2_ref), (OFF_G3, DIL_PAIRS[2][1], b3_ref)):
        for c in range(nslab):
            scr_ref[c] = proj[:, off + c * LANES: off + (c + 1) * LANES]
        for j in range(r):
            rows = [scr_ref[c, pl.ds(j, tm // r, stride=r), :] for c in range(nslab)]
            out_ref[j] = jnp.concatenate(rows, axis=1).astype(BF16)


def _in_proj(x, y, lw, tabs, tm=TOKEN_TILE):
    B, S, _ = x.shape
    has_y = y is not None
    hw = MLA_HEADS * MLA_HEAD_PAD
    r2, r3 = DIL_PAIRS[1][1], DIL_PAIRS[2][1]
    tok = lambda c: pl.BlockSpec((None, tm, c), lambda b, i: (b, i, 0))
    in_specs = [tok(D_MODEL)]
    args = [x]
    if has_y:
        in_specs.append(tok(D_MODEL))
        args.append(y)
    in_specs += [
        _const_spec((1, D_MODEL)), _const_spec((D_MODEL, BIG_COLS)),
        _const_spec((1, MLA_Q_LORA)), _const_spec((MLA_Q_LORA, 2 * hw)),
        _const_spec((1, MLA_KV_LORA)), _const_spec((MLA_KV_LORA, hw + MLA_HEADS * MLA_V)),
        pl.BlockSpec((tm, LANES), lambda b, i: (i, 0)),
        pl.BlockSpec((tm, LANES), lambda b, i: (i, 0)),
    ]
    args += [lw["norm1_g"], lw["w_big"], lw["q_norm_g"], lw["w_q"], lw["kv_norm_g"], lw["w_kv"],
             tabs[0], tabs[1]]
    out_shape, out_specs = [], []
    if has_y:
        out_shape.append(jax.ShapeDtypeStruct((B, S, D_MODEL), F32))
        out_specs.append(tok(D_MODEL))
    out_shape += [
        jax.ShapeDtypeStruct((B, S, hw), BF16), jax.ShapeDtypeStruct((B, S, hw), BF16),
        jax.ShapeDtypeStruct((B, S, MLA_HEADS * MLA_V), BF16),
        jax.ShapeDtypeStruct((B, S, B_GROUP_COLS), BF16),
        jax.ShapeDtypeStruct((B, r2, S // r2, B_GROUP_COLS), BF16),
        jax.ShapeDtypeStruct((B, r3, S // r3, B_GROUP_COLS), BF16),
        jax.ShapeDtypeStruct((B, S, B_GROUP_COLS), BF16),
    ]
    out_specs += [
        tok(hw), tok(hw), tok(MLA_HEADS * MLA_V), tok(B_GROUP_COLS),
        pl.BlockSpec((None, r2, tm // r2, B_GROUP_COLS), lambda b, i: (b, 0, i, 0)),
        pl.BlockSpec((None, r3, tm // r3, B_GROUP_COLS), lambda b, i: (b, 0, i, 0)),
        tok(B_GROUP_COLS),
    ]
    outs = pl.pallas_call(
        functools.partial(_in_proj_kernel, has_y=has_y, tm=tm),
        grid=(B, S // tm),
        in_specs=in_specs,
        out_specs=out_specs,
        out_shape=out_shape,
        scratch_shapes=[pltpu.VMEM((B_GROUP_COLS // LANES, tm, LANES), F32)],
        compiler_params=pltpu.CompilerParams(
            dimension_semantics=("parallel", "parallel"), vmem_limit_bytes=VMEM_LIMIT),
        name="in_proj",
    )(*args)
    if has_y:
        return outs[0], outs[1:]
    return x, outs


def _mla_kernel(q_ref, k_ref, v_ref, o_ref, *, tq, tk, nk):
    qs = [q_ref[:, h * MLA_HEAD_PAD:(h + 1) * MLA_HEAD_PAD] for h in range(2)]

    def body(j, carry):
        start = pl.multiple_of(j * tk, tk)
        vs = v_ref[pl.ds(start, tk), :]
        new = []
        for h in range(2):
            m, l, acc = carry[h]
            ks = k_ref[pl.ds(start, tk), h * MLA_HEAD_PAD:(h + 1) * MLA_HEAD_PAD]
            s = _dot_nt(qs[h], ks)
            mn = jnp.maximum(m, jnp.max(s, axis=-1, keepdims=True))
            alpha = jnp.exp(m - mn)
            p = jnp.exp(s - mn)
            l = alpha * l + jnp.sum(p, axis=-1, keepdims=True)
            acc = alpha * acc + _dot(p.astype(BF16), vs)
            new.append((mn, l, acc))
        return tuple(new)

    init = tuple((jnp.full((tq, 1), NEG_INF, F32), jnp.zeros((tq, 1), F32),
                  jnp.zeros((tq, LANES), F32)) for _ in range(2))
    res = lax.fori_loop(0, nk, body, init)
    outs = [res[h][2] / res[h][1] for h in range(2)]
    lane = lax.broadcasted_iota(jnp.int32, (tq, LANES), 1)
    o_ref[...] = jnp.where(lane < MLA_V, outs[0], outs[1]).astype(BF16)


def _mla_attention(q, k, v, tq=256, tk=512):
    B, S, _ = q.shape
    tk = min(tk, S)
    return pl.pallas_call(
        functools.partial(_mla_kernel, tq=tq, tk=tk, nk=S // tk),
        grid=(B, MLA_HEADS // 2, S // tq),
        in_specs=[
            pl.BlockSpec((None, tq, 2 * MLA_HEAD_PAD), lambda b, h, i: (b, i, h)),
            pl.BlockSpec((None, S, 2 * MLA_HEAD_PAD), lambda b, h, i: (b, 0, h)),
            pl.BlockSpec((None, S, 2 * MLA_V), lambda b, h, i: (b, 0, h)),
        ],
        out_specs=pl.BlockSpec((None, tq, 2 * MLA_V), lambda b, h, i: (b, i, h)),
        out_shape=jax.ShapeDtypeStruct((B, S, MLA_HEADS * MLA_V), BF16),
        compiler_params=pltpu.CompilerParams(
            dimension_semantics=("parallel", "parallel", "parallel"), vmem_limit_bytes=VMEM_LIMIT),
        name="mla_attention",
    )(q, k, v)


def _band_kernel(*refs, tq, L, W, half_w, slopes, has_sink, want_lse):
    if has_sink:
        sink_ref = refs[0]
        refs = refs[1:]
    q_ref, k_ref, v_ref = refs[:3]
    o_ref = refs[3]
    lse_ref = refs[4] if want_lse else None

    t0 = pl.program_id(1) * tq
    if W == L:
        start = 0
        k = k_ref[...]
        v = v_ref[...]
    else:
        start = pl.multiple_of(jnp.clip(t0 - half_w, 0, L - W), HEAD_DIM)
        k = k_ref[pl.ds(start, W), :]
        v = v_ref[pl.ds(start, W), :]
    q = q_ref[...]
    col = lax.broadcasted_iota(jnp.int32, (tq, W), 1)
    row = lax.broadcasted_iota(jnp.int32, (tq, W), 0)
    dist = jnp.abs(col - row + (start - t0)).astype(F32)
    valid = dist <= float(half_w)
    lane = lax.broadcasted_iota(jnp.int32, (tq, LANES), 1)
    low = lane < HEAD_DIM
    qscale = jnp.asarray(HEAD_DIM ** -0.5, BF16)

    for pair in range(2):
        sl = slice(pair * LANES, (pair + 1) * LANES)
        qp, kp, vp = q[:, sl] * qscale, k[:, sl], v[:, sl]
        outs, lses = [], []
        for hh in range(2):
            head = 2 * pair + hh
            qm = jnp.where(low if hh == 0 else jnp.logical_not(low), qp, jnp.zeros_like(qp))
            s = _dot_nt(qm, kp) - float(slopes[head]) * dist
            s = jnp.where(valid, s, NEG_INF)
            m = jnp.max(s, axis=-1, keepdims=True)
            if has_sink:
                m = jnp.maximum(m, sink_ref[head])
            p = jnp.exp(s - m)
            den = jnp.sum(p, axis=-1, keepdims=True)
            if has_sink:
                den = den + jnp.exp(sink_ref[head] - m)
            outs.append(_dot(p.astype(BF16), vp) / den)
            if want_lse:
                lses.append(jnp.broadcast_to(m + jnp.log(den), (tq, LANES)))
        o_ref[:, sl] = jnp.where(low, outs[0], outs[1]).astype(BF16)
        if want_lse:
            lse_ref[:, sl] = jnp.where(low, lses[0], lses[1])


def _band_attention(qkv, half_w, slopes, sink=None, want_lse=True, tq=256):
    N, L, _ = qkv.shape
    tq = min(tq, L)
    W = min(L, tq + 2 * half_w)
    nq = 4 * HEAD_DIM
    in_specs = [
        pl.BlockSpec((None, tq, nq), lambda n, i: (n, i, 0)),
        pl.BlockSpec((None, L, nq), lambda n, i: (n, 0, 1)),
        pl.BlockSpec((None, L, nq), lambda n, i: (n, 0, 2)),
    ]
    args = [qkv, qkv, qkv]
    if sink is not None:
        in_specs = [pl.BlockSpec(memory_space=pltpu.SMEM)] + in_specs
        args = [sink] + args
    out_shape = [jax.ShapeDtypeStruct((N, L, nq), BF16)]
    out_specs = [pl.BlockSpec((None, tq, nq), lambda n, i: (n, i, 0))]
    if want_lse:
        out_shape.append(jax.ShapeDtypeStruct((N, L, nq), F32))
        out_specs.append(pl.BlockSpec((None, tq, nq), lambda n, i: (n, i, 0)))
    outs = pl.pallas_call(
        functools.partial(_band_kernel, tq=tq, L=L, W=W, half_w=half_w,
                          slopes=tuple(float(s) for s in slopes),
                          has_sink=sink is not None, want_lse=want_lse),
        grid=(N, L // tq),
        in_specs=in_specs,
        out_specs=out_specs,
        out_shape=out_shape,
        compiler_params=pltpu.CompilerParams(
            dimension_semantics=("parallel", "parallel"), vmem_limit_bytes=VMEM_LIMIT),
        name="band_attention",
    )(*args)
    return outs


def _out_proj_kernel(x_ref, ya_ref, o1_ref, l1_ref, o2_ref, l2_ref, o3_ref, l3_ref, oc_ref,
                     wo_ref, g2_ref, wr_ref, xm_ref, h2_ref, route_ref, scr_ref, *, tm):
    nq = DIL_HEADS * HEAD_DIM
    ncs = nq // LANES
    slab = 0
    merged = []
    for r, o_ref, l_ref in ((DIL_PAIRS[1][1], o2_ref, l2_ref), (DIL_PAIRS[2][1], o3_ref, l3_ref)):
        for j in range(r):
            oj = o_ref[j].astype(F32)
            lj = l_ref[j]
            for c in range(ncs):
                scr_ref[slab + c, pl.ds(j, tm // r, stride=r), :] = oj[:, c * LANES:(c + 1) * LANES]
                scr_ref[slab + ncs + c, pl.ds(j, tm // r, stride=r), :] = lj[:, c * LANES:(c + 1) * LANES]
        on = jnp.concatenate([scr_ref[slab + c] for c in range(ncs)], axis=1)
        ln = jnp.concatenate([scr_ref[slab + ncs + c] for c in range(ncs)], axis=1)
        merged.append((on, ln))
        slab += 2 * ncs
    o1 = o1_ref[...].astype(F32)
    l1 = l1_ref[...]
    (o2, l2), (o3, l3) = merged
    mx = jnp.maximum(l1, jnp.maximum(l2, l3))
    e1, e2, e3 = jnp.exp(l1 - mx), jnp.exp(l2 - mx), jnp.exp(l3 - mx)
    yb = (e1 * o1 + e2 * o2 + e3 * o3) / (e1 + e2 + e3)

    na = MLA_HEADS * MLA_V
    y = (_dot(ya_ref[...], wo_ref[0:na, :])
         + _dot(yb.astype(BF16), wo_ref[na:na + nq, :])
         + _dot(oc_ref[...], wo_ref[na + nq:, :]))
    xm = x_ref[...] + y
    xm_ref[...] = xm
    h2 = _rms(xm, g2_ref[...]).astype(BF16)
    h2_ref[...] = h2

    logits = _dot(h2, wr_ref[...])
    lane = lax.broadcasted_iota(jnp.int32, (tm, LANES), 1).astype(F32)
    big = float(LANES)
    lg = jnp.where(lane < MOE_GROUPS, logits, NEG_INF)
    gmax = jnp.max(lg, axis=-1, keepdims=True)
    gsel = jnp.min(jnp.where(lg == gmax, lane, big), axis=-1, keepdims=True)
    gden = jnp.sum(jnp.where(lane < MOE_GROUPS, jnp.exp(lg - gmax), 0.0), axis=-1, keepdims=True)
    gprob = 1.0 / gden
    lo = MOE_GROUPS + MOE_EPG * gsel
    le = jnp.where((lane >= lo) & (lane < lo + MOE_EPG), logits, NEG_INF)
    t1 = jnp.max(le, axis=-1, keepdims=True)
    i1 = jnp.min(jnp.where(le == t1, lane, big), axis=-1, keepdims=True)
    le2 = jnp.where(lane == i1, NEG_INF, le)
    t2 = jnp.max(le2, axis=-1, keepdims=True)
    i2 = jnp.min(jnp.where(le2 == t2, lane, big), axis=-1, keepdims=True)
    ex = jnp.exp(t2 - t1)
    gate1 = gprob / (1.0 + ex)
    gate2 = gprob * ex / (1.0 + ex)
    route = jnp.where(lane == 0, i1 - MOE_GROUPS,
                      jnp.where(lane == 1, i2 - MOE_GROUPS,
                                jnp.where(lane == 2, gate1, jnp.where(lane == 3, gate2, 0.0))))
    route_ref[...] = route


def _out_proj(x, ya, b1, b2, b3, oc, lw, tm=TOKEN_TILE):
    B, S, _ = x.shape
    nq = DIL_HEADS * HEAD_DIM
    r2, r3 = DIL_PAIRS[1][1], DIL_PAIRS[2][1]
    tok = lambda c: pl.BlockSpec((None, tm, c), lambda b, i: (b, i, 0))
    res = lambda r: pl.BlockSpec((None, r, tm // r, nq), lambda b, i: (b, 0, i, 0))
    return pl.pallas_call(
        functools.partial(_out_proj_kernel, tm=tm),
        grid=(B, S // tm),
        in_specs=[tok(D_MODEL), tok(MLA_HEADS * MLA_V), tok(nq), tok(nq), res(r2), res(r2),
                  res(r3), res(r3), tok(nq),
                  _const_spec((D_MODEL, D_MODEL)), _const_spec((1, D_MODEL)),
                  _const_spec((D_MODEL, LANES))],
        out_specs=[tok(D_MODEL), tok(D_MODEL), tok(LANES)],
        out_shape=[jax.ShapeDtypeStruct((B, S, D_MODEL), F32),
                   jax.ShapeDtypeStruct((B, S, D_MODEL), BF16),
                   jax.ShapeDtypeStruct((B, S, LANES), F32)],
        scratch_shapes=[pltpu.VMEM((4 * (nq // LANES), tm, LANES), F32)],
        compiler_params=pltpu.CompilerParams(
            dimension_semantics=("parallel", "parallel"), vmem_limit_bytes=VMEM_LIMIT),
        name="out_proj_router",
    )(x, ya, b1[0], b1[1], b2[0].reshape(B, r2, S // r2, nq), b2[1].reshape(B, r2, S // r2, nq),
      b3[0].reshape(B, r3, S // r3, nq), b3[1].reshape(B, r3, S // r3, nq), oc,
      lw["w_out"], lw["norm2_g"], lw["w_router"])


def _moe_kernel(ea_ref, eb_ref, valid_ref, hs_ref, ga_ref, gb_ref, wgu_a, wd_a, wgu_b, wd_b, y_ref):
    i = pl.program_id(0)

    @pl.when(valid_ref[i] == 1)
    def _():
        h = hs_ref[...]

        def ffn(wgu_ref, wd_ref, g_ref):
            au = _dot(h, wgu_ref[...])
            a, u = au[:, :MOE_HIDDEN], au[:, MOE_HIDDEN:]
            g = g_ref[...]
            z = (a * jax.nn.sigmoid(a) * u) * jnp.concatenate([g, g], axis=1)
            return _dot(z.astype(BF16), wd_ref[...])

        y_ref[...] = (ffn(wgu_a, wd_a, ga_ref) + ffn(wgu_b, wd_b, gb_ref)).astype(BF16)

    @pl.when(valid_ref[i] == 0)
    def _():
        y_ref[...] = jnp.zeros_like(y_ref)


def _moe_experts(hs, ga, gb, ea, eb, valid, lw, tile=MOE_TILE):
    P = hs.shape[0]
    row = lambda c: pl.BlockSpec((tile, c), lambda i, ea, eb, va: (i, 0))
    wgu = lambda which: pl.BlockSpec(
        (None, D_MODEL, 2 * MOE_HIDDEN),
        (lambda i, ea, eb, va: (ea[i], 0, 0)) if which == 0 else (lambda i, ea, eb, va: (eb[i], 0, 0)))
    wd = lambda which: pl.BlockSpec(
        (None, MOE_HIDDEN, D_MODEL),
        (lambda i, ea, eb, va: (ea[i], 0, 0)) if which == 0 else (lambda i, ea, eb, va: (eb[i], 0, 0)))
    return pl.pallas_call(
        _moe_kernel,
        grid_spec=pltpu.PrefetchScalarGridSpec(
            num_scalar_prefetch=3,
            grid=(P // tile,),
            in_specs=[row(D_MODEL), row(LANES), row(LANES), wgu(0), wd(0), wgu(1), wd(1)],
            out_specs=row(D_MODEL),
        ),
        out_shape=jax.ShapeDtypeStruct((P, D_MODEL), BF16),
        compiler_params=pltpu.CompilerParams(
            dimension_semantics=("arbitrary",), vmem_limit_bytes=VMEM_LIMIT),
        name="moe_experts",
    )(ea, eb, valid, hs, ga, gb, lw["w_gu"], lw["w_d"], lw["w_gu"], lw["w_d"])


def _final_kernel(x_ref, y_ref, g_ref, o_ref):
    o_ref[...] = _rms(x_ref[...] + y_ref[...].astype(F32), g_ref[...])


def _final_norm(x, y, g, tm=512):
    B, S, _ = x.shape
    tok = pl.BlockSpec((None, tm, D_MODEL), lambda b, i: (b, i, 0))
    return pl.pallas_call(
        _final_kernel,
        grid=(B, S // tm),
        in_specs=[tok, tok, _const_spec((1, D_MODEL))],
        out_specs=tok,
        out_shape=jax.ShapeDtypeStruct((B, S, D_MODEL), F32),
        compiler_params=pltpu.CompilerParams(dimension_semantics=("parallel", "parallel")),
        name="final_norm",
    )(x, y, g)


def _rot_cols(w):
    half = w.shape[1] // 2
    return jnp.concatenate([-w[:, half:], w[:, :half]], axis=1)


def _prep_layer(p, l):
    w_in = p["w_in"][l]
    dm = w_in.shape[0]
    z = lambda n, rows=dm: jnp.zeros((rows, n), F32)
    kr = w_in[:, OFF_KRA:A_COLS]
    pc = w_in[:, A_COLS + B_COLS:]
    nq, nk = SWA_Q_HEADS * HEAD_DIM, SWA_KV_HEADS * HEAD_DIM
    dup = lambda w: jnp.concatenate([w[:, :HEAD_DIM]] * 2 + [w[:, HEAD_DIM:]] * 2, axis=1)
    w_big = jnp.concatenate([
        w_in[:, :OFF_KRA],
        z(MLA_NOPE), kr, z(MLA_HEAD_PAD - MLA_NOPE - MLA_ROPE),
        z(MLA_NOPE), _rot_cols(kr), z(MLA_HEAD_PAD - MLA_NOPE - MLA_ROPE),
        w_in[:, A_COLS:A_COLS + B_COLS],
        pc[:, :nq], dup(pc[:, nq:nq + nk]), dup(pc[:, nq + nk:]),
    ], axis=1).astype(BF16)

    w_uq = p["mla_w_uq"][l].reshape(MLA_Q_LORA, MLA_HEADS, MLA_NOPE + MLA_ROPE)
    pad = MLA_HEAD_PAD - MLA_NOPE - MLA_ROPE
    zq = lambda n: jnp.zeros((MLA_Q_LORA, MLA_HEADS, n), F32)
    pe = w_uq[:, :, MLA_NOPE:]
    pe_rot = jnp.concatenate([-pe[:, :, MLA_ROPE // 2:], pe[:, :, :MLA_ROPE // 2]], axis=2)
    wq_a = jnp.concatenate([w_uq, zq(pad)], axis=2).reshape(MLA_Q_LORA, -1)
    wq_b = jnp.concatenate([zq(MLA_NOPE), pe_rot, zq(pad)], axis=2).reshape(MLA_Q_LORA, -1)
    w_q = jnp.concatenate([wq_a, wq_b], axis=1).astype(BF16)

    w_ukv = p["mla_w_ukv"][l].reshape(MLA_KV_LORA, MLA_HEADS, MLA_NOPE + MLA_V)
    wk = jnp.concatenate([w_ukv[:, :, :MLA_NOPE],
                          jnp.zeros((MLA_KV_LORA, MLA_HEADS, MLA_HEAD_PAD - MLA_NOPE), F32)], axis=2)
    w_kv = jnp.concatenate([wk.reshape(MLA_KV_LORA, -1),
                            w_ukv[:, :, MLA_NOPE:].reshape(MLA_KV_LORA, -1)], axis=1).astype(BF16)

    w_router = jnp.concatenate([p["w_router_group"][l], p["w_router_expert"][l],
                                z(LANES - MOE_GROUPS - MOE_EXPERTS)], axis=1).astype(BF16)
    return {
        "norm1_g": p["norm1_g"][l][None, :], "w_big": w_big,
        "q_norm_g": p["mla_q_norm_g"][l][None, :], "w_q": w_q,
        "kv_norm_g": p["mla_kv_norm_g"][l][None, :], "w_kv": w_kv,
        "sink": p["swa_sink"][l].astype(F32),
        "w_out": p["w_out"][l].astype(BF16), "norm2_g": p["norm2_g"][l][None, :],
        "w_router": w_router,
        "w_gu": jnp.concatenate([p["w_gate"][l], p["w_up"][l]], axis=2).astype(BF16),
        "w_d": p["w_down"][l].astype(BF16),
    }


def _rope_tables(S):
    pos = jnp.arange(S, dtype=F32)
    freqs = ROPE_THETA ** (-jnp.arange(0, MLA_ROPE, 2, dtype=F32) / MLA_ROPE)
    ang = pos[:, None] * freqs[None, :]
    cos, sin = jnp.cos(ang), jnp.sin(ang)
    pad = MLA_HEAD_PAD - MLA_NOPE - MLA_ROPE
    cos_t = jnp.concatenate([jnp.ones((S, MLA_NOPE), F32), cos, cos, jnp.zeros((S, pad), F32)], axis=1)
    sin_t = jnp.concatenate([jnp.zeros((S, MLA_NOPE), F32), sin, sin, jnp.zeros((S, pad), F32)], axis=1)
    return cos_t, sin_t


def _pair_tables():
    ea, eb = [], []
    for g in range(MOE_GROUPS):
        for a in range(MOE_EPG):
            for b in range(a + 1, MOE_EPG):
                ea.append(g * MOE_EPG + a)
                eb.append(g * MOE_EPG + b)
    return np.asarray(ea, np.int32), np.asarray(eb, np.int32)


def _dispatch_plan(routes, tile=MOE_TILE):
    r = jnp.concatenate([x[:, :4] for x in routes], axis=0)
    T = r.shape[0]
    e1, e2 = r[:, 0].astype(jnp.int32), r[:, 1].astype(jnp.int32)
    first = e1 < e2
    a, b = jnp.minimum(e1, e2), jnp.maximum(e1, e2)
    ga = jnp.where(first, r[:, 2], r[:, 3])
    gb = jnp.where(first, r[:, 3], r[:, 2])
    la, lb = a % MOE_EPG, b % MOE_EPG
    cls = (a // MOE_EPG) * (MOE_EPG * (MOE_EPG - 1) // 2) + (la * (2 * MOE_EPG - 1 - la)) // 2 + (lb - la - 1)
    onehot = (cls[:, None] == jnp.arange(N_PAIR_CLASSES, dtype=jnp.int32)[None, :]).astype(jnp.int32)
    counts = jnp.sum(onehot, axis=0)
    rank = jnp.sum(jnp.cumsum(onehot, axis=0) * onehot, axis=1) - 1
    padded = ((counts + tile - 1) // tile) * tile
    ends = jnp.cumsum(padded)
    pos = (ends - padded)[cls] + rank
    P = T + N_PAIR_CLASSES * tile
    src = jnp.zeros((P,), jnp.int32).at[pos].set(jnp.arange(T, dtype=jnp.int32))
    ga_s = jnp.zeros((P,), F32).at[pos].set(ga)
    gb_s = jnp.zeros((P,), F32).at[pos].set(gb)
    tile_start = jnp.arange(P // tile, dtype=jnp.int32) * tile
    tcls = jnp.searchsorted(ends, tile_start, side="right").astype(jnp.int32)
    valid = (tile_start < ends[-1]).astype(jnp.int32)
    last = jnp.max(jnp.where(valid == 1, tcls, 0))
    tcls = jnp.where(valid == 1, tcls, last)
    ta, tb = _pair_tables()
    ea, eb = jnp.asarray(ta)[tcls], jnp.asarray(tb)[tcls]
    return pos, src, ga_s, gb_s, ea, eb, valid


def _moe(h2s, routes, lw):
    flat = [h.reshape(-1, D_MODEL) for h in h2s]
    pos, src, ga_s, gb_s, ea, eb, valid = _dispatch_plan([r.reshape(-1, LANES) for r in routes])
    hcat = jnp.concatenate(flat, axis=0)
    hs = hcat[src]
    P = hs.shape[0]
    ga_b = jnp.broadcast_to(ga_s[:, None], (P, LANES))
    gb_b = jnp.broadcast_to(gb_s[:, None], (P, LANES))
    y = _moe_experts(hs, ga_b, gb_b, ea, eb, valid, lw)
    outs, off = [], 0
    for h in h2s:
        n = h.shape[0] * h.shape[1]
        outs.append(y[pos[off:off + n]].reshape(h.shape))
        off += n
    return outs


def _mixers(x, y, lw, tabs, slopes):
    B, S, _ = x.shape
    x, (q, k, v, g1, g2, g3, gc) = _in_proj(x, y, lw, tabs)
    ya = _mla_attention(q, k, v)
    nb = B_GROUP_COLS
    sl_c, sl_b = slopes[:SWA_Q_HEADS], slopes[SWA_Q_HEADS:].reshape(3, DIL_HEADS)
    outs = []
    for gi, arr in enumerate((g1, g2, g3)):
        window, r = DIL_PAIRS[gi]
        outs.append(_band_attention(arr.reshape(B * r, S // r, nb), window // (2 * r), sl_b[gi] * r))
    oc = _band_attention(gc, SWA_HALF_WINDOW, sl_c, sink=lw["sink"], want_lse=False)[0]
    return _out_proj(x, ya, outs[0], outs[1], outs[2], oc, lw)


def kernel(x_prompt, x_sample, norm1_g, w_in, mla_q_norm_g, mla_w_uq, mla_kv_norm_g, mla_w_ukv, swa_sink,
           w_out, norm2_g, w_router_group, w_router_expert, w_gate, w_up, w_down, final_norm_g):
    p = dict(norm1_g=norm1_g, w_in=w_in, mla_q_norm_g=mla_q_norm_g, mla_w_uq=mla_w_uq,
             mla_kv_norm_g=mla_kv_norm_g, mla_w_ukv=mla_w_ukv, swa_sink=swa_sink, w_out=w_out,
             norm2_g=norm2_g, w_router_group=w_router_group, w_router_expert=w_router_expert,
             w_gate=w_gate, w_up=w_up, w_down=w_down)
    slopes = _alibi_slopes()
    xs = [x_prompt, x_sample]
    tabs = [_rope_tables(x.shape[1]) for x in xs]
    ys = [None, None]
    for l in range(DEPTH):
        lw = _prep_layer(p, l)
        xm, h2, route = [], [], []
        for i in range(2):
            a, b, c = _mixers(xs[i], ys[i], lw, tabs[i], slopes)
            xm.append(a)
            h2.append(b)
            route.append(c)
        xs = xm
        ys = _moe(h2, route, lw)
    g = final_norm_g[None, :]
    return tuple(_final_norm(xs[i], ys[i], g) for i in range(2))
```

```python
import functools

import numpy as np
import jax
import jax.numpy as jnp
from jax import lax
from jax.experimental import pallas as pl
from jax.experimental.pallas import tpu as pltpu

F32 = jnp.float32
BF16 = jnp.bfloat16

D_MODEL = 1024
DEPTH = 2
HEAD_DIM = 64
MLA_HEADS = 8
MLA_Q_LORA = 384
MLA_KV_LORA = 256
MLA_NOPE = 64
MLA_ROPE = 32
MLA_V = 64
ROPE_THETA = 10000.0
DIL_PAIRS = ((128, 1), (512, 4), (2048, 16))
DIL_HEADS = 4
SWA_Q_HEADS = 4
SWA_KV_HEADS = 2
SWA_HALF_WINDOW = 128
N_ALIBI_HEADS = 16
MOE_GROUPS = 4
MOE_EPG = 8
MOE_EXPERTS = 32
MOE_HIDDEN = 256
RMS_EPS = 1e-6
NEG_INF = -1e30

LANES = 128
MLA_HEAD_PAD = 128
A_COLS = MLA_Q_LORA + MLA_KV_LORA + MLA_ROPE
B_GROUP_COLS = 3 * DIL_HEADS * HEAD_DIM
B_COLS = 3 * B_GROUP_COLS
OFF_CQ = 0
OFF_CKV = MLA_Q_LORA
OFF_KRA = OFF_CKV + MLA_KV_LORA
OFF_KRB = OFF_KRA + LANES
OFF_G1 = OFF_KRB + LANES
OFF_G2 = OFF_G1 + B_GROUP_COLS
OFF_G3 = OFF_G2 + B_GROUP_COLS
OFF_GC = OFF_G3 + B_GROUP_COLS
BIG_COLS = OFF_GC + B_GROUP_COLS

MLA_Q_SCALE = float((MLA_NOPE + MLA_ROPE) ** -0.5 * np.log2(np.e))
N_PAIR_CLASSES = MOE_GROUPS * (MOE_EPG * (MOE_EPG - 1) // 2)
MOE_TILE = 256
TOKEN_TILE = 256
VMEM_LIMIT = 56 * 1024 * 1024


def _alibi_slopes():
    return 2.0 ** (-8.0 * np.arange(1, N_ALIBI_HEADS + 1, dtype=np.float64) / N_ALIBI_HEADS)


def _rms(x, g):
    return x * lax.rsqrt(jnp.mean(x * x, axis=-1, keepdims=True) + RMS_EPS) * g


def _dot(a, b):
    return jnp.dot(a, b, preferred_element_type=F32)


def _dot_nt(a, b):
    return lax.dot_general(a, b, (((1,), (1,)), ((), ())), preferred_element_type=F32)


def _const_spec(shape):
    nd = len(shape)
    return pl.BlockSpec(shape, lambda *_: (0,) * nd, pipeline_mode=pl.Buffered(1))


def _in_proj_kernel(*refs, has_y, tm):
    if has_y:
        x_ref, y_ref = refs[:2]
        refs = refs[2:]
    else:
        x_ref = refs[0]
        refs = refs[1:]
    (g1_ref, wbig_ref, qg_ref, wq_ref, kvg_ref, wkv_ref, cos_ref, sin_ref) = refs[:8]
    refs = refs[8:]
    if has_y:
        xo_ref = refs[0]
        refs = refs[1:]
    q_ref, k_ref, v_ref, b1_ref, b2_ref, b3_ref, c_ref, scr_ref = refs

    x = x_ref[...]
    if has_y:
        x = x + y_ref[...].astype(F32)
        xo_ref[...] = x
    h = _rms(x, g1_ref[...]).astype(BF16)
    proj = _dot(h, wbig_ref[...])

    cos = cos_ref[...]
    sin = sin_ref[...]
    cos8 = jnp.concatenate([cos] * MLA_HEADS, axis=1)
    sin8 = jnp.concatenate([sin] * MLA_HEADS, axis=1)
    hw = MLA_HEADS * MLA_HEAD_PAD

    cqn = _rms(proj[:, OFF_CQ:OFF_CKV], qg_ref[...]).astype(BF16)
    qa = _dot(cqn, wq_ref[...])
    q = (qa[:, :hw] * cos8 + qa[:, hw:] * sin8) * MLA_Q_SCALE
    q_ref[...] = q.astype(BF16)

    ckvn = _rms(proj[:, OFF_CKV:OFF_KRA], kvg_ref[...]).astype(BF16)
    kva = _dot(ckvn, wkv_ref[...])
    kpe = proj[:, OFF_KRA:OFF_KRB] * cos + proj[:, OFF_KRB:OFF_G1] * sin
    hl = lax.broadcasted_iota(jnp.int32, (tm, hw), 1) % MLA_HEAD_PAD
    k = kva[:, :hw] + jnp.concatenate([kpe] * MLA_HEADS, axis=1) + (hl == MLA_HEAD_PAD - 1).astype(F32)
    k_ref[...] = k.astype(BF16)
    v_ref[...] = (kva[:, hw:] + (hl >= MLA_V).astype(F32)).astype(BF16)

    b1_ref[...] = proj[:, OFF_G1:OFF_G2].astype(BF16)
    c_ref[...] = proj[:, OFF_GC:BIG_COLS].astype(BF16)

    nslab = B_GROUP_COLS // LANES
    for off, r, out_ref in ((OFF_G2, DIL_PAIRS[1][1], b2_ref), (OFF_G3, DIL_PAIRS[2][1], b3_ref)):
        for c in range(nslab):
            scr_ref[c] = proj[:, off + c * LANES: off + (c + 1) * LANES]
        for j in range(r):
            rows = [scr_ref[c, pl.ds(j, tm // r, stride=r), :] for c in range(nslab)]
            out_ref[j] = jnp.concatenate(rows, axis=1).astype(BF16)


def _in_proj(x, y, lw, tabs, tm=TOKEN_TILE):
    B, S, _ = x.shape
    has_y = y is not None
    hw = MLA_HEADS * MLA_HEAD_PAD
    r2, r3 = DIL_PAIRS[1][1], DIL_PAIRS[2][1]
    tok = lambda c: pl.BlockSpec((None, tm, c), lambda b, i: (b, i, 0))
    in_specs = [tok(D_MODEL)]
    args = [x]
    if has_y:
        in_specs.append(tok(D_MODEL))
        args.append(y)
    in_specs += [
        _const_spec((1, D_MODEL)), _const_spec((D_MODEL, BIG_COLS)),
        _const_spec((1, MLA_Q_LORA)), _const_spec((MLA_Q_LORA, 2 * hw)),
        _const_spec((1, MLA_KV_LORA)), _const_spec((MLA_KV_LORA, 2 * hw)),
        pl.BlockSpec((tm, LANES), lambda b, i: (i, 0)),
        pl.BlockSpec((tm, LANES), lambda b, i: (i, 0)),
    ]
    args += [lw["norm1_g"], lw["w_big"], lw["q_norm_g"], lw["w_q"], lw["kv_norm_g"], lw["w_kv"],
             tabs[0], tabs[1]]
    out_shape, out_specs = [], []
    if has_y:
        out_shape.append(jax.ShapeDtypeStruct((B, S, D_MODEL), F32))
        out_specs.append(tok(D_MODEL))
    out_shape += [
        jax.ShapeDtypeStruct((B, S, hw), BF16), jax.ShapeDtypeStruct((B, S, hw), BF16),
        jax.ShapeDtypeStruct((B, S, hw), BF16),
        jax.ShapeDtypeStruct((B, S, B_GROUP_COLS), BF16),
        jax.ShapeDtypeStruct((B, r2, S // r2, B_GROUP_COLS), BF16),
        jax.ShapeDtypeStruct((B, r3, S // r3, B_GROUP_COLS), BF16),
        jax.ShapeDtypeStruct((B, S, B_GROUP_COLS), BF16),
    ]
    out_specs += [
        tok(hw), tok(hw), tok(hw), tok(B_GROUP_COLS),
        pl.BlockSpec((None, r2, tm // r2, B_GROUP_COLS), lambda b, i: (b, 0, i, 0)),
        pl.BlockSpec((None, r3, tm // r3, B_GROUP_COLS), lambda b, i: (b, 0, i, 0)),
        tok(B_GROUP_COLS),
    ]
    outs = pl.pallas_call(
        functools.partial(_in_proj_kernel, has_y=has_y, tm=tm),
        grid=(B, S // tm),
        in_specs=in_specs,
        out_specs=out_specs,
        out_shape=out_shape,
        scratch_shapes=[pltpu.VMEM((B_GROUP_COLS // LANES, tm, LANES), F32)],
        compiler_params=pltpu.CompilerParams(
            dimension_semantics=("parallel", "parallel"), vmem_limit_bytes=VMEM_LIMIT),
        name="in_proj",
    )(*args)
    if has_y:
        return outs[0], outs[1:]
    return x, outs


def _mla_kernel(q_ref, k_ref, v_ref, o_ref, *, tq, tk, nk):
    hp = MLA_HEAD_PAD
    lane = lax.broadcasted_iota(jnp.int32, (tq, LANES), 1)
    low = lane < MLA_V
    qs = [q_ref[:, h * hp:(h + 1) * hp] for h in range(2)]

    def finish(accs):
        o0 = accs[0] / pltpu.roll(accs[0], MLA_V, 1)
        o1 = pltpu.roll(accs[1], MLA_V, 1) / accs[1]
        o_ref[...] = jnp.where(low, o0, o1).astype(BF16)

    qx = []
    for h in range(2):
        s0 = _dot_nt(qs[h], k_ref[0:LANES, h * hp:(h + 1) * hp])
        shift = jnp.max(s0, axis=-1, keepdims=True).astype(BF16)
        qx.append(jnp.where(lane == hp - 1, -shift, qs[h]))
    accs = [jnp.zeros((tq, LANES), F32) for _ in range(2)]
    for j in range(nk):
        for h in range(2):
            ks = k_ref[j * tk:(j + 1) * tk, h * hp:(h + 1) * hp]
            vs = v_ref[j * tk:(j + 1) * tk, h * hp:(h + 1) * hp]
            p = jnp.exp2(_dot_nt(qx[h], ks)).astype(BF16)
            accs[h] = accs[h] + _dot(p, vs)
    bad = jnp.max(jnp.where(jnp.isfinite(accs[0]) & jnp.isfinite(accs[1]), 0.0, 1.0))
    finish(accs)

    @pl.when(bad != 0.0)
    def _():
        def body(j, carry):
            start = pl.multiple_of(j * tk, tk)
            new = []
            for h in range(2):
                m, acc = carry[h]
                ks = k_ref[pl.ds(start, tk), h * hp:(h + 1) * hp]
                vs = v_ref[pl.ds(start, tk), h * hp:(h + 1) * hp]
                s = _dot_nt(qs[h], ks)
                mn = jnp.maximum(m, jnp.max(s, axis=-1, keepdims=True))
                p = jnp.exp2(s - mn).astype(BF16)
                new.append((mn, jnp.exp2(m - mn) * acc + _dot(p, vs)))
            return tuple(new)

        init = tuple((jnp.full((tq, 1), NEG_INF, F32), jnp.zeros((tq, LANES), F32)) for _ in range(2))
        res = lax.fori_loop(0, nk, body, init)
        finish([res[0][1], res[1][1]])


def _mla_attention(q, k, v, tq=512, tk=512):
    B, S, _ = q.shape
    tk = min(tk, S)
    return pl.pallas_call(
        functools.partial(_mla_kernel, tq=tq, tk=tk, nk=S // tk),
        grid=(B, MLA_HEADS // 2, S // tq),
        in_specs=[
            pl.BlockSpec((None, tq, 2 * MLA_HEAD_PAD), lambda b, h, i: (b, i, h)),
            pl.BlockSpec((None, S, 2 * MLA_HEAD_PAD), lambda b, h, i: (b, 0, h)),
            pl.BlockSpec((None, S, 2 * MLA_HEAD_PAD), lambda b, h, i: (b, 0, h)),
        ],
        out_specs=pl.BlockSpec((None, tq, 2 * MLA_V), lambda b, h, i: (b, i, h)),
        out_shape=jax.ShapeDtypeStruct((B, S, MLA_HEADS * MLA_V), BF16),
        compiler_params=pltpu.CompilerParams(
            dimension_semantics=("parallel", "parallel", "parallel"), vmem_limit_bytes=VMEM_LIMIT),
        name="mla_attention",
    )(q, k, v)


def _band_kernel(*refs, tq, L, W, half_w, slopes, has_sink, want_lse):
    if has_sink:
        sink_ref = refs[0]
        refs = refs[1:]
    q_ref, k_ref, v_ref = refs[:3]
    o_ref = refs[3]
    lse_ref = refs[4] if want_lse else None

    t0 = pl.program_id(1) * tq
    if W == L:
        start = 0
        k = k_ref[...]
        v = v_ref[...]
    else:
        start = pl.multiple_of(jnp.clip(t0 - half_w, 0, L - W), HEAD_DIM)
        k = k_ref[pl.ds(start, W), :]
        v = v_ref[pl.ds(start, W), :]
    q = q_ref[...]
    col = lax.broadcasted_iota(jnp.int32, (tq, W), 1)
    row = lax.broadcasted_iota(jnp.int32, (tq, W), 0)
    dist = jnp.abs(col - row + (start - t0)).astype(F32)
    valid = dist <= float(half_w)
    lane = lax.broadcasted_iota(jnp.int32, (tq, LANES), 1)
    low = lane < HEAD_DIM
    qscale = jnp.asarray(HEAD_DIM ** -0.5, BF16)

    for pair in range(2):
        sl = slice(pair * LANES, (pair + 1) * LANES)
        qp, kp, vp = q[:, sl] * qscale, k[:, sl], v[:, sl]
        outs, lses = [], []
        for hh in range(2):
            head = 2 * pair + hh
            qm = jnp.where(low if hh == 0 else jnp.logical_not(low), qp, jnp.zeros_like(qp))
            s = _dot_nt(qm, kp) - float(slopes[head]) * dist
            s = jnp.where(valid, s, NEG_INF)
            m = jnp.max(s, axis=-1, keepdims=True)
            if has_sink:
                m = jnp.maximum(m, sink_ref[head])
            p = jnp.exp(s - m)
            den = jnp.sum(p, axis=-1, keepdims=True)
            if has_sink:
                den = den + jnp.exp(sink_ref[head] - m)
            outs.append(_dot(p.astype(BF16), vp) / den)
            if want_lse:
                lses.append(jnp.broadcast_to(m + jnp.log(den), (tq, LANES)))
        o_ref[:, sl] = jnp.where(low, outs[0], outs[1]).astype(BF16)
        if want_lse:
            lse_ref[:, sl] = jnp.where(low, lses[0], lses[1])


def _band_attention(qkv, half_w, slopes, sink=None, want_lse=True, tq=256):
    N, L, _ = qkv.shape
    tq = min(tq, L)
    W = min(L, tq + 2 * half_w)
    nq = 4 * HEAD_DIM
    in_specs = [
        pl.BlockSpec((None, tq, nq), lambda n, i: (n, i, 0)),
        pl.BlockSpec((None, L, nq), lambda n, i: (n, 0, 1)),
        pl.BlockSpec((None, L, nq), lambda n, i: (n, 0, 2)),
    ]
    args = [qkv, qkv, qkv]
    if sink is not None:
        in_specs = [pl.BlockSpec(memory_space=pltpu.SMEM)] + in_specs
        args = [sink] + args
    out_shape = [jax.ShapeDtypeStruct((N, L, nq), BF16)]
    out_specs = [pl.BlockSpec((None, tq, nq), lambda n, i: (n, i, 0))]
    if want_lse:
        out_shape.append(jax.ShapeDtypeStruct((N, L, nq), F32))
        out_specs.append(pl.BlockSpec((None, tq, nq), lambda n, i: (n, i, 0)))
    outs = pl.pallas_call(
        functools.partial(_band_kernel, tq=tq, L=L, W=W, half_w=half_w,
                          slopes=tuple(float(s) for s in slopes),
                          has_sink=sink is not None, want_lse=want_lse),
        grid=(N, L // tq),
        in_specs=in_specs,
        out_specs=out_specs,
        out_shape=out_shape,
        compiler_params=pltpu.CompilerParams(
            dimension_semantics=("parallel", "parallel"), vmem_limit_bytes=VMEM_LIMIT),
        name="band_attention",
    )(*args)
    return outs


def _out_proj_kernel(x_ref, ya_ref, o1_ref, l1_ref, o2_ref, l2_ref, o3_ref, l3_ref, oc_ref,
                     wo_ref, g2_ref, wr_ref, xm_ref, h2_ref, route_ref, scr_ref, *, tm):
    nq = DIL_HEADS * HEAD_DIM
    ncs = nq // LANES
    slab = 0
    merged = []
    for r, o_ref, l_ref in ((DIL_PAIRS[1][1], o2_ref, l2_ref), (DIL_PAIRS[2][1], o3_ref, l3_ref)):
        for j in range(r):
            oj = o_ref[j].astype(F32)
            lj = l_ref[j]
            for c in range(ncs):
                scr_ref[slab + c, pl.ds(j, tm // r, stride=r), :] = oj[:, c * LANES:(c + 1) * LANES]
                scr_ref[slab + ncs + c, pl.ds(j, tm // r, stride=r), :] = lj[:, c * LANES:(c + 1) * LANES]
        on = jnp.concatenate([scr_ref[slab + c] for c in range(ncs)], axis=1)
        ln = jnp.concatenate([scr_ref[slab + ncs + c] for c in range(ncs)], axis=1)
        merged.append((on, ln))
        slab += 2 * ncs
    o1 = o1_ref[...].astype(F32)
    l1 = l1_ref[...]
    (o2, l2), (o3, l3) = merged
    mx = jnp.maximum(l1, jnp.maximum(l2, l3))
    e1, e2, e3 = jnp.exp(l1 - mx), jnp.exp(l2 - mx), jnp.exp(l3 - mx)
    yb = (e1 * o1 + e2 * o2 + e3 * o3) / (e1 + e2 + e3)

    na = MLA_HEADS * MLA_V
    y = (_dot(ya_ref[...], wo_ref[0:na, :])
         + _dot(yb.astype(BF16), wo_ref[na:na + nq, :])
         + _dot(oc_ref[...], wo_ref[na + nq:, :]))
    xm = x_ref[...] + y
    xm_ref[...] = xm
    h2 = _rms(xm, g2_ref[...]).astype(BF16)
    h2_ref[...] = h2

    logits = _dot(h2, wr_ref[...])
    lane = lax.broadcasted_iota(jnp.int32, (tm, LANES), 1).astype(F32)
    big = float(LANES)
    lg = jnp.where(lane < MOE_GROUPS, logits, NEG_INF)
    gmax = jnp.max(lg, axis=-1, keepdims=True)
    gsel = jnp.min(jnp.where(lg == gmax, lane, big), axis=-1, keepdims=True)
    gden = jnp.sum(jnp.where(lane < MOE_GROUPS, jnp.exp(lg - gmax), 0.0), axis=-1, keepdims=True)
    gprob = 1.0 / gden
    lo = MOE_GROUPS + MOE_EPG * gsel
    le = jnp.where((lane >= lo) & (lane < lo + MOE_EPG), logits, NEG_INF)
    t1 = jnp.max(le, axis=-1, keepdims=True)
    i1 = jnp.min(jnp.where(le == t1, lane, big), axis=-1, keepdims=True)
    le2 = jnp.where(lane == i1, NEG_INF, le)
    t2 = jnp.max(le2, axis=-1, keepdims=True)
    i2 = jnp.min(jnp.where(le2 == t2, lane, big), axis=-1, keepdims=True)
    ex = jnp.exp(t2 - t1)
    gate1 = gprob / (1.0 + ex)
    gate2 = gprob * ex / (1.0 + ex)
    route = jnp.where(lane == 0, i1 - MOE_GROUPS,
                      jnp.where(lane == 1, i2 - MOE_GROUPS,
                                jnp.where(lane == 2, gate1, jnp.where(lane == 3, gate2, 0.0))))
    route_ref[...] = route


def _out_proj(x, ya, b1, b2, b3, oc, lw, tm=TOKEN_TILE):
    B, S, _ = x.shape
    nq = DIL_HEADS * HEAD_DIM
    r2, r3 = DIL_PAIRS[1][1], DIL_PAIRS[2][1]
    tok = lambda c: pl.BlockSpec((None, tm, c), lambda b, i: (b, i, 0))
    res = lambda r: pl.BlockSpec((None, r, tm // r, nq), lambda b, i: (b, 0, i, 0))
    return pl.pallas_call(
        functools.partial(_out_proj_kernel, tm=tm),
        grid=(B, S // tm),
        in_specs=[tok(D_MODEL), tok(MLA_HEADS * MLA_V), tok(nq), tok(nq), res(r2), res(r2),
                  res(r3), res(r3), tok(nq),
                  _const_spec((D_MODEL, D_MODEL)), _const_spec((1, D_MODEL)),
                  _const_spec((D_MODEL, LANES))],
        out_specs=[tok(D_MODEL), tok(D_MODEL), tok(LANES)],
        out_shape=[jax.ShapeDtypeStruct((B, S, D_MODEL), F32),
                   jax.ShapeDtypeStruct((B, S, D_MODEL), BF16),
                   jax.ShapeDtypeStruct((B, S, LANES), F32)],
        scratch_shapes=[pltpu.VMEM((4 * (nq // LANES), tm, LANES), F32)],
        compiler_params=pltpu.CompilerParams(
            dimension_semantics=("parallel", "parallel"), vmem_limit_bytes=VMEM_LIMIT),
        name="out_proj_router",
    )(x, ya, b1[0], b1[1], b2[0].reshape(B, r2, S // r2, nq), b2[1].reshape(B, r2, S // r2, nq),
      b3[0].reshape(B, r3, S // r3, nq), b3[1].reshape(B, r3, S // r3, nq), oc,
      lw["w_out"], lw["norm2_g"], lw["w_router"])


def _moe_kernel(ea_ref, eb_ref, valid_ref, hs_ref, ga_ref, gb_ref, wgu_a, wd_a, wgu_b, wd_b, y_ref):
    i = pl.program_id(0)

    @pl.when(valid_ref[i] == 1)
    def _():
        h = hs_ref[...]

        def ffn(wgu_ref, wd_ref, g_ref):
            au = _dot(h, wgu_ref[...])
            a, u = au[:, :MOE_HIDDEN], au[:, MOE_HIDDEN:]
            g = g_ref[...]
            z = (a * jax.nn.sigmoid(a) * u) * jnp.concatenate([g, g], axis=1)
            return _dot(z.astype(BF16), wd_ref[...])

        y_ref[...] = (ffn(wgu_a, wd_a, ga_ref) + ffn(wgu_b, wd_b, gb_ref)).astype(BF16)

    @pl.when(valid_ref[i] == 0)
    def _():
        y_ref[...] = jnp.zeros_like(y_ref)


def _moe_experts(hs, ga, gb, ea, eb, valid, lw, tile=MOE_TILE):
    P = hs.shape[0]
    row = lambda c: pl.BlockSpec((tile, c), lambda i, ea, eb, va: (i, 0))
    wgu = lambda which: pl.BlockSpec(
        (None, D_MODEL, 2 * MOE_HIDDEN),
        (lambda i, ea, eb, va: (ea[i], 0, 0)) if which == 0 else (lambda i, ea, eb, va: (eb[i], 0, 0)))
    wd = lambda which: pl.BlockSpec(
        (None, MOE_HIDDEN, D_MODEL),
        (lambda i, ea, eb, va: (ea[i], 0, 0)) if which == 0 else (lambda i, ea, eb, va: (eb[i], 0, 0)))
    return pl.pallas_call(
        _moe_kernel,
        grid_spec=pltpu.PrefetchScalarGridSpec(
            num_scalar_prefetch=3,
            grid=(P // tile,),
            in_specs=[row(D_MODEL), row(LANES), row(LANES), wgu(0), wd(0), wgu(1), wd(1)],
            out_specs=row(D_MODEL),
        ),
        out_shape=jax.ShapeDtypeStruct((P, D_MODEL), BF16),
        compiler_params=pltpu.CompilerParams(
            dimension_semantics=("arbitrary",), vmem_limit_bytes=VMEM_LIMIT),
        name="moe_experts",
    )(ea, eb, valid, hs, ga, gb, lw["w_gu"], lw["w_d"], lw["w_gu"], lw["w_d"])


def _final_kernel(x_ref, y_ref, g_ref, o_ref):
    o_ref[...] = _rms(x_ref[...] + y_ref[...].astype(F32), g_ref[...])


def _final_norm(x, y, g, tm=512):
    B, S, _ = x.shape
    tok = pl.BlockSpec((None, tm, D_MODEL), lambda b, i: (b, i, 0))
    return pl.pallas_call(
        _final_kernel,
        grid=(B, S // tm),
        in_specs=[tok, tok, _const_spec((1, D_MODEL))],
        out_specs=tok,
        out_shape=jax.ShapeDtypeStruct((B, S, D_MODEL), F32),
        compiler_params=pltpu.CompilerParams(dimension_semantics=("parallel", "parallel")),
        name="final_norm",
    )(x, y, g)


def _rot_cols(w):
    half = w.shape[1] // 2
    return jnp.concatenate([-w[:, half:], w[:, :half]], axis=1)


def _prep_layer(p, l):
    w_in = p["w_in"][l]
    dm = w_in.shape[0]
    z = lambda n, rows=dm: jnp.zeros((rows, n), F32)
    kr = w_in[:, OFF_KRA:A_COLS]
    pc = w_in[:, A_COLS + B_COLS:]
    nq, nk = SWA_Q_HEADS * HEAD_DIM, SWA_KV_HEADS * HEAD_DIM
    dup = lambda w: jnp.concatenate([w[:, :HEAD_DIM]] * 2 + [w[:, HEAD_DIM:]] * 2, axis=1)
    w_big = jnp.concatenate([
        w_in[:, :OFF_KRA],
        z(MLA_NOPE), kr, z(MLA_HEAD_PAD - MLA_NOPE - MLA_ROPE),
        z(MLA_NOPE), _rot_cols(kr), z(MLA_HEAD_PAD - MLA_NOPE - MLA_ROPE),
        w_in[:, A_COLS:A_COLS + B_COLS],
        pc[:, :nq], dup(pc[:, nq:nq + nk]), dup(pc[:, nq + nk:]),
    ], axis=1).astype(BF16)

    w_uq = p["mla_w_uq"][l].reshape(MLA_Q_LORA, MLA_HEADS, MLA_NOPE + MLA_ROPE)
    pad = MLA_HEAD_PAD - MLA_NOPE - MLA_ROPE
    zq = lambda n: jnp.zeros((MLA_Q_LORA, MLA_HEADS, n), F32)
    pe = w_uq[:, :, MLA_NOPE:]
    pe_rot = jnp.concatenate([-pe[:, :, MLA_ROPE // 2:], pe[:, :, :MLA_ROPE // 2]], axis=2)
    wq_a = jnp.concatenate([w_uq, zq(pad)], axis=2).reshape(MLA_Q_LORA, -1)
    wq_b = jnp.concatenate([zq(MLA_NOPE), pe_rot, zq(pad)], axis=2).reshape(MLA_Q_LORA, -1)
    w_q = jnp.concatenate([wq_a, wq_b], axis=1).astype(BF16)

    w_ukv = p["mla_w_ukv"][l].reshape(MLA_KV_LORA, MLA_HEADS, MLA_NOPE + MLA_V)
    zk = jnp.zeros((MLA_KV_LORA, MLA_HEADS, MLA_HEAD_PAD - MLA_NOPE), F32)
    wk = jnp.concatenate([w_ukv[:, :, :MLA_NOPE], zk], axis=2)
    wv = jnp.concatenate([w_ukv[:, :, MLA_NOPE:], zk], axis=2)
    w_kv = jnp.concatenate([wk.reshape(MLA_KV_LORA, -1), wv.reshape(MLA_KV_LORA, -1)], axis=1).astype(BF16)

    w_router = jnp.concatenate([p["w_router_group"][l], p["w_router_expert"][l],
                                z(LANES - MOE_GROUPS - MOE_EXPERTS)], axis=1).astype(BF16)
    return {
        "norm1_g": p["norm1_g"][l][None, :], "w_big": w_big,
        "q_norm_g": p["mla_q_norm_g"][l][None, :], "w_q": w_q,
        "kv_norm_g": p["mla_kv_norm_g"][l][None, :], "w_kv": w_kv,
        "sink": p["swa_sink"][l].astype(F32),
        "w_out": p["w_out"][l].astype(BF16), "norm2_g": p["norm2_g"][l][None, :],
        "w_router": w_router,
        "w_gu": jnp.concatenate([p["w_gate"][l], p["w_up"][l]], axis=2).astype(BF16),
        "w_d": p["w_down"][l].astype(BF16),
    }


def _rope_tables(S):
    pos = jnp.arange(S, dtype=F32)
    freqs = ROPE_THETA ** (-jnp.arange(0, MLA_ROPE, 2, dtype=F32) / MLA_ROPE)
    ang = pos[:, None] * freqs[None, :]
    cos, sin = jnp.cos(ang), jnp.sin(ang)
    pad = MLA_HEAD_PAD - MLA_NOPE - MLA_ROPE
    cos_t = jnp.concatenate([jnp.ones((S, MLA_NOPE), F32), cos, cos, jnp.zeros((S, pad), F32)], axis=1)
    sin_t = jnp.concatenate([jnp.zeros((S, MLA_NOPE), F32), sin, sin, jnp.zeros((S, pad), F32)], axis=1)
    return cos_t, sin_t


def _pair_tables():
    ea, eb = [], []
    for g in range(MOE_GROUPS):
        for a in range(MOE_EPG):
            for b in range(a + 1, MOE_EPG):
                ea.append(g * MOE_EPG + a)
                eb.append(g * MOE_EPG + b)
    return np.asarray(ea, np.int32), np.asarray(eb, np.int32)


def _dispatch_plan(routes, tile=MOE_TILE):
    r = jnp.concatenate([x[:, :4] for x in routes], axis=0)
    T = r.shape[0]
    e1, e2 = r[:, 0].astype(jnp.int32), r[:, 1].astype(jnp.int32)
    first = e1 < e2
    a, b = jnp.minimum(e1, e2), jnp.maximum(e1, e2)
    ga = jnp.where(first, r[:, 2], r[:, 3])
    gb = jnp.where(first, r[:, 3], r[:, 2])
    la, lb = a % MOE_EPG, b % MOE_EPG
    cls = (a // MOE_EPG) * (MOE_EPG * (MOE_EPG - 1) // 2) + (la * (2 * MOE_EPG - 1 - la)) // 2 + (lb - la - 1)
    onehot = (cls[:, None] == jnp.arange(N_PAIR_CLASSES, dtype=jnp.int32)[None, :]).astype(jnp.int32)
    counts = jnp.sum(onehot, axis=0)
    rank = jnp.sum(jnp.cumsum(onehot, axis=0) * onehot, axis=1) - 1
    padded = ((counts + tile - 1) // tile) * tile
    ends = jnp.cumsum(padded)
    pos = (ends - padded)[cls] + rank
    P = T + N_PAIR_CLASSES * tile
    src = jnp.zeros((P,), jnp.int32).at[pos].set(jnp.arange(T, dtype=jnp.int32))
    ga_s = jnp.zeros((P,), F32).at[pos].set(ga)
    gb_s = jnp.zeros((P,), F32).at[pos].set(gb)
    tile_start = jnp.arange(P // tile, dtype=jnp.int32) * tile
    tcls = jnp.sum((tile_start[:, None] >= ends[None, :]).astype(jnp.int32), axis=1)
    valid = (tile_start < ends[-1]).astype(jnp.int32)
    last = jnp.max(jnp.where(valid == 1, tcls, 0))
    tcls = jnp.where(valid == 1, tcls, last)
    ta, tb = _pair_tables()
    ea, eb = jnp.asarray(ta)[tcls], jnp.asarray(tb)[tcls]
    return pos, src, ga_s, gb_s, ea, eb, valid


def _moe(h2s, routes, lw):
    flat = [h.reshape(-1, D_MODEL) for h in h2s]
    pos, src, ga_s, gb_s, ea, eb, valid = _dispatch_plan([r.reshape(-1, LANES) for r in routes])
    hcat = jnp.concatenate(flat, axis=0)
    hs = hcat[src]
    P = hs.shape[0]
    ga_b = jnp.broadcast_to(ga_s[:, None], (P, LANES))
    gb_b = jnp.broadcast_to(gb_s[:, None], (P, LANES))
    y = _moe_experts(hs, ga_b, gb_b, ea, eb, valid, lw)
    outs, off = [], 0
    for h in h2s:
        n = h.shape[0] * h.shape[1]
        outs.append(y[pos[off:off + n]].reshape(h.shape))
        off += n
    return outs


def _mixers(x, y, lw, tabs, slopes):
    B, S, _ = x.shape
    x, (q, k, v, g1, g2, g3, gc) = _in_proj(x, y, lw, tabs)
    ya = _mla_attention(q, k, v)
    nb = B_GROUP_COLS
    sl_c, sl_b = slopes[:SWA_Q_HEADS], slopes[SWA_Q_HEADS:].reshape(3, DIL_HEADS)
    outs = []
    for gi, arr in enumerate((g1, g2, g3)):
        window, r = DIL_PAIRS[gi]
        outs.append(_band_attention(arr.reshape(B * r, S // r, nb), window // (2 * r), sl_b[gi] * r))
    oc = _band_attention(gc, SWA_HALF_WINDOW, sl_c, sink=lw["sink"], want_lse=False)[0]
    return _out_proj(x, ya, outs[0], outs[1], outs[2], oc, lw)


def kernel(x_prompt, x_sample, norm1_g, w_in, mla_q_norm_g, mla_w_uq, mla_kv_norm_g, mla_w_ukv, swa_sink,
           w_out, norm2_g, w_router_group, w_router_expert, w_gate, w_up, w_down, final_norm_g):
    p = dict(norm1_g=norm1_g, w_in=w_in, mla_q_norm_g=mla_q_norm_g, mla_w_uq=mla_w_uq,
             mla_kv_norm_g=mla_kv_norm_g, mla_w_ukv=mla_w_ukv, swa_sink=swa_sink, w_out=w_out,
             norm2_g=norm2_g, w_router_group=w_router_group, w_router_expert=w_router_expert,
             w_gate=w_gate, w_up=w_up, w_down=w_down)
    slopes = _alibi_slopes()
    xs = [x_prompt, x_sample]
    tabs = [_rope_tables(x.shape[1]) for x in xs]
    ys = [None, None]
    for l in range(DEPTH):
        lw = _prep_layer(p, l)
        xm, h2, route = [], [], []
        for i in range(2):
            a, b, c = _mixers(xs[i], ys[i], lw, tabs[i], slopes)
            xm.append(a)
            h2.append(b)
            route.append(c)
        xs = xm
        ys = _moe(h2, route, lw)
    g = final_norm_g[None, :]
    return tuple(_final_norm(xs[i], ys[i], g) for i in range(2))
```

```python
import functools

import numpy as np
import jax
import jax.numpy as jnp
from jax import lax
from jax.experimental import pallas as pl
from jax.experimental.pallas import tpu as pltpu

F32 = jnp.float32
BF16 = jnp.bfloat16

D_MODEL = 1024
DEPTH = 2
HEAD_DIM = 64
MLA_HEADS = 8
MLA_Q_LORA = 384
MLA_KV_LORA = 256
MLA_NOPE = 64
MLA_ROPE = 32
MLA_V = 64
ROPE_THETA = 10000.0
DIL_PAIRS = ((128, 1), (512, 4), (2048, 16))
DIL_HEADS = 4
SWA_Q_HEADS = 4
SWA_KV_HEADS = 2
SWA_HALF_WINDOW = 128
N_ALIBI_HEADS = 16
MOE_GROUPS = 4
MOE_EPG = 8
MOE_EXPERTS = 32
MOE_HIDDEN = 256
RMS_EPS = 1e-6
NEG_INF = -1e30

LANES = 128
MLA_HEAD_PAD = 128
A_COLS = MLA_Q_LORA + MLA_KV_LORA + MLA_ROPE
B_GROUP_COLS = 3 * DIL_HEADS * HEAD_DIM
B_COLS = 3 * B_GROUP_COLS
OFF_CQ = 0
OFF_CKV = MLA_Q_LORA
OFF_KRA = OFF_CKV + MLA_KV_LORA
OFF_KRB = OFF_KRA + LANES
OFF_G1 = OFF_KRB + LANES
OFF_G2 = OFF_G1 + B_GROUP_COLS
OFF_G3 = OFF_G2 + B_GROUP_COLS
OFF_GC = OFF_G3 + B_GROUP_COLS
BIG_COLS = OFF_GC + B_GROUP_COLS

LOG2E = float(np.log2(np.e))
LN2 = float(np.log(2.0))
BAND_Q_SCALE = float(HEAD_DIM ** -0.5 * np.log2(np.e))
MLA_Q_SCALE = float((MLA_NOPE + MLA_ROPE) ** -0.5 * np.log2(np.e))
PAIRS_PER_GROUP = MOE_EPG * (MOE_EPG - 1) // 2
N_PAIR_CLASSES = MOE_GROUPS * PAIRS_PER_GROUP
ROW_COLS = D_MODEL + 2 * LANES
ROW_STEP = 1024
MOE_TILE = 256
TOKEN_TILE = 256
VMEM_LIMIT = 56 * 1024 * 1024


def _alibi_slopes():
    return 2.0 ** (-8.0 * np.arange(1, N_ALIBI_HEADS + 1, dtype=np.float64) / N_ALIBI_HEADS)


def _rms(x, g):
    return x * lax.rsqrt(jnp.mean(x * x, axis=-1, keepdims=True) + RMS_EPS) * g


def _dot(a, b):
    return jnp.dot(a, b, preferred_element_type=F32)


def _dot_nt(a, b):
    return lax.dot_general(a, b, (((1,), (1,)), ((), ())), preferred_element_type=F32)


def _const_spec(shape):
    nd = len(shape)
    return pl.BlockSpec(shape, lambda *_: (0,) * nd, pipeline_mode=pl.Buffered(1))


def _in_proj_kernel(*refs, has_y, tm):
    if has_y:
        x_ref, y_ref = refs[:2]
        refs = refs[2:]
    else:
        x_ref = refs[0]
        refs = refs[1:]
    (g1_ref, wbig_ref, qg_ref, wq_ref, kvg_ref, wkv_ref, cos_ref, sin_ref) = refs[:8]
    refs = refs[8:]
    if has_y:
        xo_ref = refs[0]
        refs = refs[1:]
    q_ref, k_ref, v_ref, b1_ref, b2_ref, b3_ref, c_ref, scr_ref = refs

    x = x_ref[...]
    if has_y:
        x = x + y_ref[...].astype(F32)
        xo_ref[...] = x
    h = _rms(x, g1_ref[...]).astype(BF16)
    proj = _dot(h, wbig_ref[...])

    cos = cos_ref[...]
    sin = sin_ref[...]
    cos8 = jnp.concatenate([cos] * MLA_HEADS, axis=1)
    sin8 = jnp.concatenate([sin] * MLA_HEADS, axis=1)
    hw = MLA_HEADS * MLA_HEAD_PAD

    cqn = _rms(proj[:, OFF_CQ:OFF_CKV], qg_ref[...]).astype(BF16)
    qa = _dot(cqn, wq_ref[...])
    q = (qa[:, :hw] * cos8 + qa[:, hw:] * sin8) * MLA_Q_SCALE
    q_ref[...] = q.astype(BF16)

    ckvn = _rms(proj[:, OFF_CKV:OFF_KRA], kvg_ref[...]).astype(BF16)
    kva = _dot(ckvn, wkv_ref[...])
    kpe = proj[:, OFF_KRA:OFF_KRB] * cos + proj[:, OFF_KRB:OFF_G1] * sin
    hl = lax.broadcasted_iota(jnp.int32, (tm, hw), 1) % MLA_HEAD_PAD
    k = kva[:, :hw] + jnp.concatenate([kpe] * MLA_HEADS, axis=1) + (hl == MLA_HEAD_PAD - 1).astype(F32)
    k_ref[...] = k.astype(BF16)
    v_ref[...] = (kva[:, hw:] + (hl >= MLA_V).astype(F32)).astype(BF16)

    nslab = B_GROUP_COLS // LANES
    nq_slab = DIL_HEADS * HEAD_DIM // LANES

    def slab(off, c):
        s = proj[:, off + c * LANES: off + (c + 1) * LANES]
        return s * BAND_Q_SCALE if c < nq_slab else s

    b1_ref[...] = jnp.concatenate([slab(OFF_G1, c) for c in range(nslab)], axis=1).astype(BF16)
    c_ref[...] = jnp.concatenate([slab(OFF_GC, c) for c in range(nslab)], axis=1).astype(BF16)

    for off, r, out_ref in ((OFF_G2, DIL_PAIRS[1][1], b2_ref), (OFF_G3, DIL_PAIRS[2][1], b3_ref)):
        for c in range(nslab):
            scr_ref[c] = slab(off, c)
        for j in range(r):
            rows = [scr_ref[c, pl.ds(j, tm // r, stride=r), :] for c in range(nslab)]
            out_ref[j] = jnp.concatenate(rows, axis=1).astype(BF16)


def _in_proj(x, y, lw, tabs, tm=TOKEN_TILE):
    B, S, _ = x.shape
    has_y = y is not None
    hw = MLA_HEADS * MLA_HEAD_PAD
    r2, r3 = DIL_PAIRS[1][1], DIL_PAIRS[2][1]
    tok = lambda c: pl.BlockSpec((None, tm, c), lambda b, i: (b, i, 0))
    in_specs = [tok(D_MODEL)]
    args = [x]
    if has_y:
        in_specs.append(tok(D_MODEL))
        args.append(y)
    in_specs += [
        _const_spec((1, D_MODEL)), _const_spec((D_MODEL, BIG_COLS)),
        _const_spec((1, MLA_Q_LORA)), _const_spec((MLA_Q_LORA, 2 * hw)),
        _const_spec((1, MLA_KV_LORA)), _const_spec((MLA_KV_LORA, 2 * hw)),
        pl.BlockSpec((tm, LANES), lambda b, i: (i, 0)),
        pl.BlockSpec((tm, LANES), lambda b, i: (i, 0)),
    ]
    args += [lw["norm1_g"], lw["w_big"], lw["q_norm_g"], lw["w_q"], lw["kv_norm_g"], lw["w_kv"],
             tabs[0], tabs[1]]
    out_shape, out_specs = [], []
    if has_y:
        out_shape.append(jax.ShapeDtypeStruct((B, S, D_MODEL), F32))
        out_specs.append(tok(D_MODEL))
    out_shape += [
        jax.ShapeDtypeStruct((B, S, hw), BF16), jax.ShapeDtypeStruct((B, S, hw), BF16),
        jax.ShapeDtypeStruct((B, S, hw), BF16),
        jax.ShapeDtypeStruct((B, S, B_GROUP_COLS), BF16),
        jax.ShapeDtypeStruct((B, r2, S // r2, B_GROUP_COLS), BF16),
        jax.ShapeDtypeStruct((B, r3, S // r3, B_GROUP_COLS), BF16),
        jax.ShapeDtypeStruct((B, S, B_GROUP_COLS), BF16),
    ]
    out_specs += [
        tok(hw), tok(hw), tok(hw), tok(B_GROUP_COLS),
        pl.BlockSpec((None, r2, tm // r2, B_GROUP_COLS), lambda b, i: (b, 0, i, 0)),
        pl.BlockSpec((None, r3, tm // r3, B_GROUP_COLS), lambda b, i: (b, 0, i, 0)),
        tok(B_GROUP_COLS),
    ]
    outs = pl.pallas_call(
        functools.partial(_in_proj_kernel, has_y=has_y, tm=tm),
        grid=(B, S // tm),
        in_specs=in_specs,
        out_specs=out_specs,
        out_shape=out_shape,
        scratch_shapes=[pltpu.VMEM((B_GROUP_COLS // LANES, tm, LANES), F32)],
        compiler_params=pltpu.CompilerParams(
            dimension_semantics=("parallel", "parallel"), vmem_limit_bytes=VMEM_LIMIT),
        name="in_proj",
    )(*args)
    if has_y:
        return outs[0], outs[1:]
    return x, outs


def _mla_kernel(q_ref, k_ref, v_ref, o_ref, *, tq, tk, nk):
    hp = MLA_HEAD_PAD
    lane = lax.broadcasted_iota(jnp.int32, (tq, LANES), 1)
    low = lane < MLA_V
    qs = [q_ref[:, h * hp:(h + 1) * hp] for h in range(2)]

    def finish(accs):
        o0 = accs[0] / pltpu.roll(accs[0], MLA_V, 1)
        o1 = pltpu.roll(accs[1], MLA_V, 1) / accs[1]
        o_ref[...] = jnp.where(low, o0, o1).astype(BF16)

    qx = []
    for h in range(2):
        s0 = _dot_nt(qs[h], k_ref[0:LANES, h * hp:(h + 1) * hp])
        shift = jnp.max(s0, axis=-1, keepdims=True).astype(BF16)
        qx.append(jnp.where(lane == hp - 1, -shift, qs[h]))
    accs = [jnp.zeros((tq, LANES), F32) for _ in range(2)]
    for j in range(nk):
        for h in range(2):
            ks = k_ref[j * tk:(j + 1) * tk, h * hp:(h + 1) * hp]
            vs = v_ref[j * tk:(j + 1) * tk, h * hp:(h + 1) * hp]
            p = jnp.exp2(_dot_nt(qx[h], ks)).astype(BF16)
            accs[h] = accs[h] + _dot(p, vs)
    bad = jnp.max(jnp.where(jnp.isfinite(accs[0]) & jnp.isfinite(accs[1]), 0.0, 1.0))
    finish(accs)

    @pl.when(bad != 0.0)
    def _():
        def body(j, carry):
            start = pl.multiple_of(j * tk, tk)
            new = []
            for h in range(2):
                m, acc = carry[h]
                ks = k_ref[pl.ds(start, tk), h * hp:(h + 1) * hp]
                vs = v_ref[pl.ds(start, tk), h * hp:(h + 1) * hp]
                s = _dot_nt(qs[h], ks)
                mn = jnp.maximum(m, jnp.max(s, axis=-1, keepdims=True))
                p = jnp.exp2(s - mn).astype(BF16)
                new.append((mn, jnp.exp2(m - mn) * acc + _dot(p, vs)))
            return tuple(new)

        init = tuple((jnp.full((tq, 1), NEG_INF, F32), jnp.zeros((tq, LANES), F32)) for _ in range(2))
        res = lax.fori_loop(0, nk, body, init)
        finish([res[0][1], res[1][1]])


def _mla_attention(q, k, v, tq=512, tk=512):
    B, S, _ = q.shape
    tk = min(tk, S)
    return pl.pallas_call(
        functools.partial(_mla_kernel, tq=tq, tk=tk, nk=S // tk),
        grid=(B, MLA_HEADS // 2, S // tq),
        in_specs=[
            pl.BlockSpec((None, tq, 2 * MLA_HEAD_PAD), lambda b, h, i: (b, i, h)),
            pl.BlockSpec((None, S, 2 * MLA_HEAD_PAD), lambda b, h, i: (b, 0, h)),
            pl.BlockSpec((None, S, 2 * MLA_HEAD_PAD), lambda b, h, i: (b, 0, h)),
        ],
        out_specs=pl.BlockSpec((None, tq, 2 * MLA_V), lambda b, h, i: (b, i, h)),
        out_shape=jax.ShapeDtypeStruct((B, S, MLA_HEADS * MLA_V), BF16),
        compiler_params=pltpu.CompilerParams(
            dimension_semantics=("parallel", "parallel", "parallel"), vmem_limit_bytes=VMEM_LIMIT),
        name="mla_attention",
    )(q, k, v)


def _band_kernel(*refs, tq, L, W, half_w, has_sink, want_lse):
    refs = refs[1:]
    if has_sink:
        sink_ref = refs[0]
        refs = refs[1:]
    q_ref, k_ref, v_ref, bias_ref = refs[:4]
    o_ref = refs[4]
    lse_ref = refs[5] if want_lse else None

    if W == L:
        k = k_ref[...]
        v = v_ref[...]
    else:
        start = pl.multiple_of(jnp.clip(pl.program_id(1) * tq - half_w, 0, L - W), HEAD_DIM)
        k = k_ref[pl.ds(start, W), :]
        v = v_ref[pl.ds(start, W), :]
    q = q_ref[...]
    lane = lax.broadcasted_iota(jnp.int32, (tq, LANES), 1)
    low = lane < HEAD_DIM
    lane_w = lax.broadcasted_iota(jnp.int32, (W, LANES), 1)
    low_w = lane_w < HEAD_DIM
    one = jnp.ones((), BF16)

    def run(exact):
        bad = jnp.zeros((), F32)
        for pair in range(2):
            sl = slice(pair * LANES, (pair + 1) * LANES)
            qp, kp, vp = q[:, sl], k[:, sl], v[:, sl]
            outs, lses = [], []
            for hh in range(2):
                head = 2 * pair + hh
                mine = low if hh == 0 else jnp.logical_not(low)
                qm = jnp.where(mine, qp, jnp.zeros_like(qp))
                s = _dot_nt(qm, kp) + bias_ref[head]
                sk = sink_ref[head] * LOG2E if has_sink else None
                if exact:
                    m = jnp.max(s, axis=-1, keepdims=True)
                    if has_sink:
                        m = jnp.maximum(m, sk)
                    s = s - m
                    sk = sk - m if has_sink else None
                p = jnp.exp2(s).astype(BF16)
                vx = jnp.where(low_w, vp, one) if hh == 0 else jnp.where(low_w, one, vp)
                acc = _dot(p, vx)
                if has_sink:
                    acc = acc + jnp.where(mine, 0.0, jnp.exp2(sk))
                den = pltpu.roll(acc, HEAD_DIM, 1)
                outs.append(acc / den)
                if want_lse:
                    lses.append(jnp.log(den) + m * LN2 if exact else jnp.log(den))
                if not exact:
                    good = jnp.isfinite(acc) & (jnp.where(mine, den, acc) >= 2.0 ** -64)
                    bad = jnp.maximum(bad, jnp.max(jnp.where(good, 0.0, 1.0)))
            o_ref[:, sl] = jnp.where(low, outs[0], outs[1]).astype(BF16)
            if want_lse:
                lse_ref[:, sl] = jnp.where(low, lses[0], lses[1])
        return bad

    bad = run(exact=False)

    @pl.when(bad != 0.0)
    def _():
        run(exact=True)


def _band_bias(tq, L, W, half_w, slopes):
    offs = [int(np.clip(i * tq - half_w, 0, L - W)) - i * tq for i in range(L // tq)]
    uniq = sorted(set(offs))
    var = np.asarray([uniq.index(o) for o in offs], np.int32)
    rel = (np.arange(W)[None, :] - np.arange(tq)[:, None])[None] + np.asarray(uniq)[:, None, None]
    dist = np.abs(rel).astype(np.float64)
    sl = np.asarray(slopes, np.float64)[None, :, None, None]
    bias = np.where(dist[:, None] <= half_w, -sl * dist[:, None] * LOG2E, NEG_INF)
    return jnp.asarray(var), jnp.asarray(bias, F32)


def _band_attention(qkv, half_w, slopes, sink=None, want_lse=True, tq=256):
    N, L, _ = qkv.shape
    tq = min(tq, L)
    W = min(L, tq + 2 * half_w)
    nq = 4 * HEAD_DIM
    var, bias = _band_bias(tq, L, W, half_w, slopes)
    in_specs = [
        pl.BlockSpec((None, tq, nq), lambda n, i, var: (n, i, 0)),
        pl.BlockSpec((None, L, nq), lambda n, i, var: (n, 0, 1)),
        pl.BlockSpec((None, L, nq), lambda n, i, var: (n, 0, 2)),
        pl.BlockSpec((None, 4, tq, W), lambda n, i, var: (var[i], 0, 0, 0)),
    ]
    args = [qkv, qkv, qkv, bias]
    if sink is not None:
        in_specs = [pl.BlockSpec(memory_space=pltpu.SMEM)] + in_specs
        args = [sink] + args
    out_shape = [jax.ShapeDtypeStruct((N, L, nq), BF16)]
    out_specs = [pl.BlockSpec((None, tq, nq), lambda n, i, var: (n, i, 0))]
    if want_lse:
        out_shape.append(jax.ShapeDtypeStruct((N, L, nq), F32))
        out_specs.append(pl.BlockSpec((None, tq, nq), lambda n, i, var: (n, i, 0)))
    outs = pl.pallas_call(
        functools.partial(_band_kernel, tq=tq, L=L, W=W, half_w=half_w,
                          has_sink=sink is not None, want_lse=want_lse),
        grid_spec=pltpu.PrefetchScalarGridSpec(
            num_scalar_prefetch=1, grid=(N, L // tq), in_specs=in_specs, out_specs=out_specs),
        out_shape=out_shape,
        compiler_params=pltpu.CompilerParams(
            dimension_semantics=("parallel", "parallel"), vmem_limit_bytes=VMEM_LIMIT),
        name="band_attention",
    )(var, *args)
    return outs


def _out_proj_kernel(x_ref, ya_ref, o1_ref, l1_ref, o2_ref, l2_ref, o3_ref, l3_ref, oc_ref,
                     wo_ref, g2_ref, wr_ref, cnt0_ref, xm_ref, hg_ref, meta_ref, cnt_ref,
                     scr_ref, cnt_scr, *, tm):
    nq = DIL_HEADS * HEAD_DIM
    ncs = nq // LANES
    slab = 0
    merged = []
    for r, o_ref, l_ref in ((DIL_PAIRS[1][1], o2_ref, l2_ref), (DIL_PAIRS[2][1], o3_ref, l3_ref)):
        for j in range(r):
            oj = o_ref[j].astype(F32)
            lj = l_ref[j]
            for c in range(ncs):
                scr_ref[slab + c, pl.ds(j, tm // r, stride=r), :] = oj[:, c * LANES:(c + 1) * LANES]
                scr_ref[slab + ncs + c, pl.ds(j, tm // r, stride=r), :] = lj[:, c * LANES:(c + 1) * LANES]
        on = jnp.concatenate([scr_ref[slab + c] for c in range(ncs)], axis=1)
        ln = jnp.concatenate([scr_ref[slab + ncs + c] for c in range(ncs)], axis=1)
        merged.append((on, ln))
        slab += 2 * ncs
    o1 = o1_ref[...].astype(F32)
    l1 = l1_ref[...]
    (o2, l2), (o3, l3) = merged
    mx = jnp.maximum(l1, jnp.maximum(l2, l3))
    e1, e2, e3 = jnp.exp(l1 - mx), jnp.exp(l2 - mx), jnp.exp(l3 - mx)
    yb = (e1 * o1 + e2 * o2 + e3 * o3) / (e1 + e2 + e3)

    na = MLA_HEADS * MLA_V
    y = (_dot(ya_ref[...], wo_ref[0:na, :])
         + _dot(yb.astype(BF16), wo_ref[na:na + nq, :])
         + _dot(oc_ref[...], wo_ref[na + nq:, :]))
    xm = x_ref[...] + y
    xm_ref[...] = xm
    h2f = _rms(xm, g2_ref[...])
    h2 = h2f.astype(BF16)

    logits = _dot(h2, wr_ref[...])
    lane = lax.broadcasted_iota(jnp.int32, (tm, LANES), 1).astype(F32)
    big = float(LANES)
    lg = jnp.where(lane < MOE_GROUPS, logits, NEG_INF)
    gmax = jnp.max(lg, axis=-1, keepdims=True)
    gsel = jnp.min(jnp.where(lg == gmax, lane, big), axis=-1, keepdims=True)
    gden = jnp.sum(jnp.where(lane < MOE_GROUPS, jnp.exp(lg - gmax), 0.0), axis=-1, keepdims=True)
    gprob = 1.0 / gden
    lo = MOE_GROUPS + MOE_EPG * gsel
    le = jnp.where((lane >= lo) & (lane < lo + MOE_EPG), logits, NEG_INF)
    t1 = jnp.max(le, axis=-1, keepdims=True)
    i1 = jnp.min(jnp.where(le == t1, lane, big), axis=-1, keepdims=True)
    le2 = jnp.where(lane == i1, NEG_INF, le)
    t2 = jnp.max(le2, axis=-1, keepdims=True)
    i2 = jnp.min(jnp.where(le2 == t2, lane, big), axis=-1, keepdims=True)
    ex = jnp.exp(t2 - t1)
    gate1 = gprob / (1.0 + ex)
    gate2 = gprob * ex / (1.0 + ex)
    e1, e2 = i1 - MOE_GROUPS, i2 - MOE_GROUPS
    first = e1 < e2
    la = jnp.minimum(e1, e2) - MOE_EPG * gsel
    lb = jnp.maximum(e1, e2) - MOE_EPG * gsel
    cls = gsel * PAIRS_PER_GROUP + la * (2 * MOE_EPG - 1 - la) * 0.5 + (lb - la - 1.0)
    onehot = lane == cls

    @pl.when((pl.program_id(0) == 0) & (pl.program_id(1) == 0))
    def _():
        cnt_scr[...] = cnt0_ref[...]

    tri = (lax.broadcasted_iota(jnp.int32, (tm, tm), 0) >= lax.broadcasted_iota(jnp.int32, (tm, tm), 1))
    prefix = _dot(tri.astype(BF16), onehot.astype(BF16))
    base = cnt_scr[...]
    rank = jnp.sum(jnp.where(onehot, prefix + base, 0.0), axis=-1, keepdims=True) - 1.0
    cnt_scr[...] = base + prefix[tm - 1:tm, :]
    cnt_ref[...] = cnt_scr[...]
    meta_ref[...] = jnp.where(lane == 0, cls, jnp.where(lane == 1, rank, 0.0))

    hg_ref[:, 0:D_MODEL] = h2f
    hg_ref[:, D_MODEL:D_MODEL + LANES] = jnp.broadcast_to(jnp.where(first, gate1, gate2), (tm, LANES))
    hg_ref[:, D_MODEL + LANES:] = jnp.broadcast_to(jnp.where(first, gate2, gate1), (tm, LANES))


def _out_proj(x, ya, b1, b2, b3, oc, cnt0, lw, tm=2 * TOKEN_TILE):
    B, S, _ = x.shape
    nq = DIL_HEADS * HEAD_DIM
    r2, r3 = DIL_PAIRS[1][1], DIL_PAIRS[2][1]
    tok = lambda c: pl.BlockSpec((None, tm, c), lambda b, i: (b, i, 0))
    res = lambda r: pl.BlockSpec((None, r, tm // r, nq), lambda b, i: (b, 0, i, 0))
    cnt_spec = pl.BlockSpec((1, LANES), lambda b, i: (0, 0))
    return pl.pallas_call(
        functools.partial(_out_proj_kernel, tm=tm),
        grid=(B, S // tm),
        in_specs=[tok(D_MODEL), tok(MLA_HEADS * MLA_V), tok(nq), tok(nq), res(r2), res(r2),
                  res(r3), res(r3), tok(nq),
                  _const_spec((D_MODEL, D_MODEL)), _const_spec((1, D_MODEL)),
                  _const_spec((D_MODEL, LANES)), cnt_spec],
        out_specs=[tok(D_MODEL), tok(ROW_COLS), tok(LANES), cnt_spec],
        out_shape=[jax.ShapeDtypeStruct((B, S, D_MODEL), F32),
                   jax.ShapeDtypeStruct((B, S, ROW_COLS), F32),
                   jax.ShapeDtypeStruct((B, S, LANES), F32),
                   jax.ShapeDtypeStruct((1, LANES), F32)],
        scratch_shapes=[pltpu.VMEM((4 * (nq // LANES), tm, LANES), F32), pltpu.VMEM((1, LANES), F32)],
        compiler_params=pltpu.CompilerParams(
            dimension_semantics=("arbitrary", "arbitrary"), vmem_limit_bytes=VMEM_LIMIT),
        name="out_proj_router",
    )(x, ya, b1[0], b1[1], b2[0].reshape(B, r2, S // r2, nq), b2[1].reshape(B, r2, S // r2, nq),
      b3[0].reshape(B, r3, S // r3, nq), b3[1].reshape(B, r3, S // r3, nq), oc,
      lw["w_out"], lw["norm2_g"], lw["w_router"], cnt0)


def _row_copy(src_ref, src_row, dst_ref, dst_row, sem):
    return pltpu.make_async_copy(src_ref.at[pl.ds(src_row, 1)], dst_ref.at[pl.ds(dst_row, 1)], sem)


def _dispatch_kernel(pos_ref, h_ref, init_ref, out_ref, sem):
    del init_ref
    n = h_ref.shape[0]

    def start(r, c):
        _row_copy(h_ref, r, out_ref, pos_ref[0, r], sem).start()
        return c

    def wait(r, c):
        _row_copy(h_ref, 0, out_ref, 0, sem).wait()
        return c

    lax.fori_loop(0, n, start, 0, unroll=8)
    lax.fori_loop(0, n, wait, 0)


def _dispatch_rows(hg, pos, buf, rows=ROW_STEP):
    T, C = hg.shape
    return pl.pallas_call(
        _dispatch_kernel,
        grid=(T // rows,),
        in_specs=[pl.BlockSpec((None, 1, rows), lambda i: (i, 0, 0), memory_space=pltpu.SMEM),
                  pl.BlockSpec((rows, C), lambda i: (i, 0)),
                  pl.BlockSpec(memory_space=pl.ANY)],
        out_specs=pl.BlockSpec(memory_space=pl.ANY),
        out_shape=jax.ShapeDtypeStruct(buf.shape, buf.dtype),
        scratch_shapes=[pltpu.SemaphoreType.DMA(())],
        input_output_aliases={2: 0},
        compiler_params=pltpu.CompilerParams(
            dimension_semantics=("arbitrary",), vmem_limit_bytes=VMEM_LIMIT),
        name="moe_dispatch",
    )(pos.reshape(T // rows, 1, rows), hg, buf)


def _gather_kernel(pos_ref, y_ref, out_ref, sem):
    n = out_ref.shape[0]

    def start(r, c):
        _row_copy(y_ref, pos_ref[0, r], out_ref, r, sem).start()
        return c

    def wait(r, c):
        _row_copy(y_ref, 0, out_ref, 0, sem).wait()
        return c

    lax.fori_loop(0, n, start, 0, unroll=8)
    lax.fori_loop(0, n, wait, 0)


def _gather_rows(y, pos, rows=ROW_STEP):
    T = pos.shape[0]
    C = y.shape[1]
    return pl.pallas_call(
        _gather_kernel,
        grid=(T // rows,),
        in_specs=[pl.BlockSpec((None, 1, rows), lambda i: (i, 0, 0), memory_space=pltpu.SMEM),
                  pl.BlockSpec(memory_space=pl.ANY)],
        out_specs=pl.BlockSpec((rows, C), lambda i: (i, 0)),
        out_shape=jax.ShapeDtypeStruct((T, C), y.dtype),
        scratch_shapes=[pltpu.SemaphoreType.DMA(())],
        compiler_params=pltpu.CompilerParams(
            dimension_semantics=("arbitrary",), vmem_limit_bytes=VMEM_LIMIT),
        name="moe_gather",
    )(pos.reshape(T // rows, 1, rows), y)


def _moe_kernel(ea_ref, eb_ref, valid_ref, hs_ref, wgu_a, wd_a, wgu_b, wd_b, y_ref):
    i = pl.program_id(0)

    @pl.when(valid_ref[i] == 1)
    def _():
        h = hs_ref[:, 0:D_MODEL].astype(BF16)

        def ffn(wgu_ref, wd_ref, g):
            au = _dot(h, wgu_ref[...])
            a, u = au[:, :MOE_HIDDEN], au[:, MOE_HIDDEN:]
            z = (a * jax.nn.sigmoid(a) * u) * jnp.concatenate([g, g], axis=1)
            return _dot(z.astype(BF16), wd_ref[...])

        y_ref[...] = (ffn(wgu_a, wd_a, hs_ref[:, D_MODEL:D_MODEL + LANES])
                      + ffn(wgu_b, wd_b, hs_ref[:, D_MODEL + LANES:]))

    @pl.when(valid_ref[i] == 0)
    def _():
        y_ref[...] = jnp.zeros_like(y_ref)


def _moe_experts(hs, ea, eb, valid, lw, tile=MOE_TILE):
    P = hs.shape[0]
    row = lambda c: pl.BlockSpec((tile, c), lambda i, ea, eb, va: (i, 0))
    wgu = lambda which: pl.BlockSpec(
        (None, D_MODEL, 2 * MOE_HIDDEN),
        (lambda i, ea, eb, va: (ea[i], 0, 0)) if which == 0 else (lambda i, ea, eb, va: (eb[i], 0, 0)))
    wd = lambda which: pl.BlockSpec(
        (None, MOE_HIDDEN, D_MODEL),
        (lambda i, ea, eb, va: (ea[i], 0, 0)) if which == 0 else (lambda i, ea, eb, va: (eb[i], 0, 0)))
    return pl.pallas_call(
        _moe_kernel,
        grid_spec=pltpu.PrefetchScalarGridSpec(
            num_scalar_prefetch=3,
            grid=(P // tile,),
            in_specs=[row(ROW_COLS), wgu(0), wd(0), wgu(1), wd(1)],
            out_specs=row(D_MODEL),
        ),
        out_shape=jax.ShapeDtypeStruct((P, D_MODEL), F32),
        compiler_params=pltpu.CompilerParams(
            dimension_semantics=("arbitrary",), vmem_limit_bytes=VMEM_LIMIT),
        name="moe_experts",
    )(ea, eb, valid, hs, lw["w_gu"], lw["w_d"], lw["w_gu"], lw["w_d"])


def _final_kernel(x_ref, y_ref, g_ref, o_ref):
    o_ref[...] = _rms(x_ref[...] + y_ref[...].astype(F32), g_ref[...])


def _final_norm(x, y, g, tm=512):
    B, S, _ = x.shape
    tok = pl.BlockSpec((None, tm, D_MODEL), lambda b, i: (b, i, 0))
    return pl.pallas_call(
        _final_kernel,
        grid=(B, S // tm),
        in_specs=[tok, tok, _const_spec((1, D_MODEL))],
        out_specs=tok,
        out_shape=jax.ShapeDtypeStruct((B, S, D_MODEL), F32),
        compiler_params=pltpu.CompilerParams(dimension_semantics=("parallel", "parallel")),
        name="final_norm",
    )(x, y, g)


def _rot_cols(w):
    half = w.shape[1] // 2
    return jnp.concatenate([-w[:, half:], w[:, :half]], axis=1)


def _prep_layer(p, l):
    w_in = p["w_in"][l]
    dm = w_in.shape[0]
    z = lambda n, rows=dm: jnp.zeros((rows, n), F32)
    kr = w_in[:, OFF_KRA:A_COLS]
    pc = w_in[:, A_COLS + B_COLS:]
    nq, nk = SWA_Q_HEADS * HEAD_DIM, SWA_KV_HEADS * HEAD_DIM
    dup = lambda w: jnp.concatenate([w[:, :HEAD_DIM]] * 2 + [w[:, HEAD_DIM:]] * 2, axis=1)
    w_big = jnp.concatenate([
        w_in[:, :OFF_KRA],
        z(MLA_NOPE), kr, z(MLA_HEAD_PAD - MLA_NOPE - MLA_ROPE),
        z(MLA_NOPE), _rot_cols(kr), z(MLA_HEAD_PAD - MLA_NOPE - MLA_ROPE),
        w_in[:, A_COLS:A_COLS + B_COLS],
        pc[:, :nq], dup(pc[:, nq:nq + nk]), dup(pc[:, nq + nk:]),
    ], axis=1).astype(BF16)

    w_uq = p["mla_w_uq"][l].reshape(MLA_Q_LORA, MLA_HEADS, MLA_NOPE + MLA_ROPE)
    pad = MLA_HEAD_PAD - MLA_NOPE - MLA_ROPE
    zq = lambda n: jnp.zeros((MLA_Q_LORA, MLA_HEADS, n), F32)
    pe = w_uq[:, :, MLA_NOPE:]
    pe_rot = jnp.concatenate([-pe[:, :, MLA_ROPE // 2:], pe[:, :, :MLA_ROPE // 2]], axis=2)
    wq_a = jnp.concatenate([w_uq, zq(pad)], axis=2).reshape(MLA_Q_LORA, -1)
    wq_b = jnp.concatenate([zq(MLA_NOPE), pe_rot, zq(pad)], axis=2).reshape(MLA_Q_LORA, -1)
    w_q = jnp.concatenate([wq_a, wq_b], axis=1).astype(BF16)

    w_ukv = p["mla_w_ukv"][l].reshape(MLA_KV_LORA, MLA_HEADS, MLA_NOPE + MLA_V)
    zk = jnp.zeros((MLA_KV_LORA, MLA_HEADS, MLA_HEAD_PAD - MLA_NOPE), F32)
    wk = jnp.concatenate([w_ukv[:, :, :MLA_NOPE], zk], axis=2)
    wv = jnp.concatenate([w_ukv[:, :, MLA_NOPE:], zk], axis=2)
    w_kv = jnp.concatenate([wk.reshape(MLA_KV_LORA, -1), wv.reshape(MLA_KV_LORA, -1)], axis=1).astype(BF16)

    w_router = jnp.concatenate([p["w_router_group"][l], p["w_router_expert"][l],
                                z(LANES - MOE_GROUPS - MOE_EXPERTS)], axis=1).astype(BF16)
    return {
        "norm1_g": p["norm1_g"][l][None, :], "w_big": w_big,
        "q_norm_g": p["mla_q_norm_g"][l][None, :], "w_q": w_q,
        "kv_norm_g": p["mla_kv_norm_g"][l][None, :], "w_kv": w_kv,
        "sink": p["swa_sink"][l].astype(F32),
        "w_out": p["w_out"][l].astype(BF16), "norm2_g": p["norm2_g"][l][None, :],
        "w_router": w_router,
        "w_gu": jnp.concatenate([p["w_gate"][l], p["w_up"][l]], axis=2).astype(BF16),
        "w_d": p["w_down"][l].astype(BF16),
    }


def _rope_tables(S):
    pos = jnp.arange(S, dtype=F32)
    freqs = ROPE_THETA ** (-jnp.arange(0, MLA_ROPE, 2, dtype=F32) / MLA_ROPE)
    ang = pos[:, None] * freqs[None, :]
    cos, sin = jnp.cos(ang), jnp.sin(ang)
    pad = MLA_HEAD_PAD - MLA_NOPE - MLA_ROPE
    cos_t = jnp.concatenate([jnp.ones((S, MLA_NOPE), F32), cos, cos, jnp.zeros((S, pad), F32)], axis=1)
    sin_t = jnp.concatenate([jnp.zeros((S, MLA_NOPE), F32), sin, sin, jnp.zeros((S, pad), F32)], axis=1)
    return cos_t, sin_t


def _pair_tables():
    ea, eb = [], []
    for g in range(MOE_GROUPS):
        for a in range(MOE_EPG):
            for b in range(a + 1, MOE_EPG):
                ea.append(g * MOE_EPG + a)
                eb.append(g * MOE_EPG + b)
    return np.asarray(ea, np.int32), np.asarray(eb, np.int32)


def _moe(hgs, metas, counts, lw, tile=MOE_TILE):
    T = sum(h.shape[0] * h.shape[1] for h in hgs)
    P = T + N_PAIR_CLASSES * tile
    cnt = counts[0, :N_PAIR_CLASSES].astype(jnp.int32)
    padded = ((cnt + tile - 1) // tile) * tile
    ends = jnp.cumsum(padded)
    starts = ends - padded
    tile_start = jnp.arange(P // tile, dtype=jnp.int32) * tile
    tcls = jnp.sum((tile_start[:, None] >= ends[None, :]).astype(jnp.int32), axis=1)
    valid = (tile_start < ends[-1]).astype(jnp.int32)
    tcls = jnp.where(valid == 1, tcls, jnp.max(jnp.where(valid == 1, tcls, 0)))
    ta, tb = _pair_tables()
    ea, eb = jnp.asarray(ta)[tcls], jnp.asarray(tb)[tcls]

    poss = []
    buf = jnp.zeros((P, ROW_COLS), F32)
    for hg, meta in zip(hgs, metas):
        m = meta.reshape(-1, LANES)
        pos = starts[m[:, 0].astype(jnp.int32)] + m[:, 1].astype(jnp.int32)
        poss.append(pos)
        buf = _dispatch_rows(hg.reshape(-1, ROW_COLS), pos, buf)
    y = _moe_experts(buf, ea, eb, valid, lw)
    return [_gather_rows(y, pos).reshape(hg.shape[0], hg.shape[1], D_MODEL) for hg, pos in zip(hgs, poss)]


def _mixers(x, y, cnt0, lw, tabs, slopes):
    B, S, _ = x.shape
    x, (q, k, v, g1, g2, g3, gc) = _in_proj(x, y, lw, tabs)
    ya = _mla_attention(q, k, v)
    nb = B_GROUP_COLS
    sl_c, sl_b = slopes[:SWA_Q_HEADS], slopes[SWA_Q_HEADS:].reshape(3, DIL_HEADS)
    outs = []
    for gi, arr in enumerate((g1, g2, g3)):
        window, r = DIL_PAIRS[gi]
        outs.append(_band_attention(arr.reshape(B * r, S // r, nb), window // (2 * r), sl_b[gi] * r))
    oc = _band_attention(gc, SWA_HALF_WINDOW, sl_c, sink=lw["sink"], want_lse=False)[0]
    return _out_proj(x, ya, outs[0], outs[1], outs[2], oc, cnt0, lw)


def kernel(x_prompt, x_sample, norm1_g, w_in, mla_q_norm_g, mla_w_uq, mla_kv_norm_g, mla_w_ukv, swa_sink,
           w_out, norm2_g, w_router_group, w_router_expert, w_gate, w_up, w_down, final_norm_g):
    p = dict(norm1_g=norm1_g, w_in=w_in, mla_q_norm_g=mla_q_norm_g, mla_w_uq=mla_w_uq,
             mla_kv_norm_g=mla_kv_norm_g, mla_w_ukv=mla_w_ukv, swa_sink=swa_sink, w_out=w_out,
             norm2_g=norm2_g, w_router_group=w_router_group, w_router_expert=w_router_expert,
             w_gate=w_gate, w_up=w_up, w_down=w_down)
    slopes = _alibi_slopes()
    xs = [x_prompt, x_sample]
    tabs = [_rope_tables(x.shape[1]) for x in xs]
    ys = [None, None]
    for l in range(DEPTH):
        lw = _prep_layer(p, l)
        xm, hg, meta = [], [], []
        cnt = jnp.zeros((1, LANES), F32)
        for i in range(2):
            a, b, c, cnt = _mixers(xs[i], ys[i], cnt, lw, tabs[i], slopes)
            xm.append(a)
            hg.append(b)
            meta.append(c)
        xs = xm
        ys = _moe(hg, meta, cnt, lw)
    g = final_norm_g[None, :]
    return tuple(_final_norm(xs[i], ys[i], g) for i in range(2))
```

```python
import functools

import numpy as np
import jax
import jax.numpy as jnp
from jax import lax
from jax.experimental import pallas as pl
from jax.experimental.pallas import tpu as pltpu

F32 = jnp.float32
BF16 = jnp.bfloat16

D_MODEL = 1024
DEPTH = 2
HEAD_DIM = 64
MLA_HEADS = 8
MLA_Q_LORA = 384
MLA_KV_LORA = 256
MLA_NOPE = 64
MLA_ROPE = 32
MLA_V = 64
ROPE_THETA = 10000.0
DIL_PAIRS = ((128, 1), (512, 4), (2048, 16))
DIL_HEADS = 4
SWA_Q_HEADS = 4
SWA_KV_HEADS = 2
SWA_HALF_WINDOW = 128
N_ALIBI_HEADS = 16
MOE_GROUPS = 4
MOE_EPG = 8
MOE_EXPERTS = 32
MOE_HIDDEN = 256
RMS_EPS = 1e-6
NEG_INF = -1e30

LANES = 128
MLA_HEAD_PAD = 128
A_COLS = MLA_Q_LORA + MLA_KV_LORA + MLA_ROPE
B_GROUP_COLS = 3 * DIL_HEADS * HEAD_DIM
B_COLS = 3 * B_GROUP_COLS
OFF_CQ = 0
OFF_CKV = MLA_Q_LORA
OFF_KRA = OFF_CKV + MLA_KV_LORA
OFF_KRB = OFF_KRA + LANES
OFF_G1 = OFF_KRB + LANES
OFF_G2 = OFF_G1 + B_GROUP_COLS
OFF_G3 = OFF_G2 + B_GROUP_COLS
OFF_GC = OFF_G3 + B_GROUP_COLS
BIG_COLS = OFF_GC + B_GROUP_COLS

LOG2E = float(np.log2(np.e))
LN2 = float(np.log(2.0))
BAND_Q_SCALE = float(HEAD_DIM ** -0.5 * np.log2(np.e))
MLA_Q_SCALE = float((MLA_NOPE + MLA_ROPE) ** -0.5 * np.log2(np.e))
PAIRS_PER_GROUP = MOE_EPG * (MOE_EPG - 1) // 2
N_PAIR_CLASSES = MOE_GROUPS * PAIRS_PER_GROUP
ROW_CHUNKS = D_MODEL // LANES + 2
OUT_CHUNKS = D_MODEL // LANES
ROW_STEP = 1024
MOE_TILE = 256
TOKEN_TILE = 256
VMEM_LIMIT = 56 * 1024 * 1024


def _alibi_slopes():
    return 2.0 ** (-8.0 * np.arange(1, N_ALIBI_HEADS + 1, dtype=np.float64) / N_ALIBI_HEADS)


def _rms(x, g):
    return x * lax.rsqrt(jnp.mean(x * x, axis=-1, keepdims=True) + RMS_EPS) * g


def _dot(a, b):
    return jnp.dot(a, b, preferred_element_type=F32)


def _dot_nt(a, b):
    return lax.dot_general(a, b, (((1,), (1,)), ((), ())), preferred_element_type=F32)


def _const_spec(shape):
    nd = len(shape)
    return pl.BlockSpec(shape, lambda *_: (0,) * nd, pipeline_mode=pl.Buffered(1))


def _in_proj_kernel(*refs, has_y, tm):
    if has_y:
        x_ref, y_ref = refs[:2]
        refs = refs[2:]
    else:
        x_ref = refs[0]
        refs = refs[1:]
    (g1_ref, wbig_ref, qg_ref, wq_ref, kvg_ref, wkv_ref, cos_ref, sin_ref) = refs[:8]
    refs = refs[8:]
    if has_y:
        xo_ref = refs[0]
        refs = refs[1:]
    q_ref, k_ref, v_ref, b1_ref, b2_ref, b3_ref, c_ref, scr_ref = refs

    x = x_ref[...]
    if has_y:
        x = x + _load_rows(y_ref, tm, OUT_CHUNKS)
        xo_ref[...] = x
    h = _rms(x, g1_ref[...]).astype(BF16)
    proj = _dot(h, wbig_ref[...])

    cos = cos_ref[...]
    sin = sin_ref[...]
    cos8 = jnp.concatenate([cos] * MLA_HEADS, axis=1)
    sin8 = jnp.concatenate([sin] * MLA_HEADS, axis=1)
    hw = MLA_HEADS * MLA_HEAD_PAD

    cqn = _rms(proj[:, OFF_CQ:OFF_CKV], qg_ref[...]).astype(BF16)
    qa = _dot(cqn, wq_ref[...])
    q = (qa[:, :hw] * cos8 + qa[:, hw:] * sin8) * MLA_Q_SCALE
    q_ref[...] = q.astype(BF16)

    ckvn = _rms(proj[:, OFF_CKV:OFF_KRA], kvg_ref[...]).astype(BF16)
    kva = _dot(ckvn, wkv_ref[...])
    kpe = proj[:, OFF_KRA:OFF_KRB] * cos + proj[:, OFF_KRB:OFF_G1] * sin
    hl = lax.broadcasted_iota(jnp.int32, (tm, hw), 1) % MLA_HEAD_PAD
    k = kva[:, :hw] + jnp.concatenate([kpe] * MLA_HEADS, axis=1) + (hl == MLA_HEAD_PAD - 1).astype(F32)
    k_ref[...] = k.astype(BF16)
    v_ref[...] = (kva[:, hw:] + (hl >= MLA_V).astype(F32)).astype(BF16)

    nslab = B_GROUP_COLS // LANES
    nq_slab = DIL_HEADS * HEAD_DIM // LANES

    def slab(off, c):
        s = proj[:, off + c * LANES: off + (c + 1) * LANES]
        return s * BAND_Q_SCALE if c < nq_slab else s

    b1_ref[...] = jnp.concatenate([slab(OFF_G1, c) for c in range(nslab)], axis=1).astype(BF16)
    c_ref[...] = jnp.concatenate([slab(OFF_GC, c) for c in range(nslab)], axis=1).astype(BF16)

    for off, r, out_ref in ((OFF_G2, DIL_PAIRS[1][1], b2_ref), (OFF_G3, DIL_PAIRS[2][1], b3_ref)):
        for c in range(nslab):
            scr_ref[c] = slab(off, c)
        for j in range(r):
            rows = [scr_ref[c, pl.ds(j, tm // r, stride=r), :] for c in range(nslab)]
            out_ref[j] = jnp.concatenate(rows, axis=1).astype(BF16)


def _in_proj(x, y, lw, tabs, tm=TOKEN_TILE):
    B, S, _ = x.shape
    has_y = y is not None
    hw = MLA_HEADS * MLA_HEAD_PAD
    r2, r3 = DIL_PAIRS[1][1], DIL_PAIRS[2][1]
    tok = lambda c: pl.BlockSpec((None, tm, c), lambda b, i: (b, i, 0))
    in_specs = [tok(D_MODEL)]
    args = [x]
    if has_y:
        in_specs.append(pl.BlockSpec((tm * OUT_CHUNKS, LANES), lambda b, i: (b * (S // tm) + i, 0)))
        args.append(y)
    in_specs += [
        _const_spec((1, D_MODEL)), _const_spec((D_MODEL, BIG_COLS)),
        _const_spec((1, MLA_Q_LORA)), _const_spec((MLA_Q_LORA, 2 * hw)),
        _const_spec((1, MLA_KV_LORA)), _const_spec((MLA_KV_LORA, 2 * hw)),
        pl.BlockSpec((tm, LANES), lambda b, i: (i, 0)),
        pl.BlockSpec((tm, LANES), lambda b, i: (i, 0)),
    ]
    args += [lw["norm1_g"], lw["w_big"], lw["q_norm_g"], lw["w_q"], lw["kv_norm_g"], lw["w_kv"],
             tabs[0], tabs[1]]
    out_shape, out_specs = [], []
    if has_y:
        out_shape.append(jax.ShapeDtypeStruct((B, S, D_MODEL), F32))
        out_specs.append(tok(D_MODEL))
    out_shape += [
        jax.ShapeDtypeStruct((B, S, hw), BF16), jax.ShapeDtypeStruct((B, S, hw), BF16),
        jax.ShapeDtypeStruct((B, S, hw), BF16),
        jax.ShapeDtypeStruct((B, S, B_GROUP_COLS), BF16),
        jax.ShapeDtypeStruct((B, r2, S // r2, B_GROUP_COLS), BF16),
        jax.ShapeDtypeStruct((B, r3, S // r3, B_GROUP_COLS), BF16),
        jax.ShapeDtypeStruct((B, S, B_GROUP_COLS), BF16),
    ]
    out_specs += [
        tok(hw), tok(hw), tok(hw), tok(B_GROUP_COLS),
        pl.BlockSpec((None, r2, tm // r2, B_GROUP_COLS), lambda b, i: (b, 0, i, 0)),
        pl.BlockSpec((None, r3, tm // r3, B_GROUP_COLS), lambda b, i: (b, 0, i, 0)),
        tok(B_GROUP_COLS),
    ]
    outs = pl.pallas_call(
        functools.partial(_in_proj_kernel, has_y=has_y, tm=tm),
        grid=(B, S // tm),
        in_specs=in_specs,
        out_specs=out_specs,
        out_shape=out_shape,
        scratch_shapes=[pltpu.VMEM((B_GROUP_COLS // LANES, tm, LANES), F32)],
        compiler_params=pltpu.CompilerParams(
            dimension_semantics=("parallel", "parallel"), vmem_limit_bytes=VMEM_LIMIT),
        name="in_proj",
    )(*args)
    if has_y:
        return outs[0], outs[1:]
    return x, outs


def _mla_kernel(q_ref, k_ref, v_ref, o_ref, *, tq, tk, nk):
    hp = MLA_HEAD_PAD
    lane = lax.broadcasted_iota(jnp.int32, (tq, LANES), 1)
    low = lane < MLA_V
    qs = [q_ref[:, h * hp:(h + 1) * hp] for h in range(2)]

    def finish(accs):
        o0 = accs[0] / pltpu.roll(accs[0], MLA_V, 1)
        o1 = pltpu.roll(accs[1], MLA_V, 1) / accs[1]
        o_ref[...] = jnp.where(low, o0, o1).astype(BF16)

    qx = []
    for h in range(2):
        s0 = _dot_nt(qs[h], k_ref[0:LANES, h * hp:(h + 1) * hp])
        shift = jnp.max(s0, axis=-1, keepdims=True).astype(BF16)
        qx.append(jnp.where(lane == hp - 1, -shift, qs[h]))
    accs = [jnp.zeros((tq, LANES), F32) for _ in range(2)]
    for j in range(nk):
        for h in range(2):
            ks = k_ref[j * tk:(j + 1) * tk, h * hp:(h + 1) * hp]
            vs = v_ref[j * tk:(j + 1) * tk, h * hp:(h + 1) * hp]
            p = jnp.exp2(_dot_nt(qx[h], ks)).astype(BF16)
            accs[h] = accs[h] + _dot(p, vs)
    bad = jnp.max(jnp.where(jnp.isfinite(accs[0]) & jnp.isfinite(accs[1]), 0.0, 1.0))
    finish(accs)

    @pl.when(bad != 0.0)
    def _():
        def body(j, carry):
            start = pl.multiple_of(j * tk, tk)
            new = []
            for h in range(2):
                m, acc = carry[h]
                ks = k_ref[pl.ds(start, tk), h * hp:(h + 1) * hp]
                vs = v_ref[pl.ds(start, tk), h * hp:(h + 1) * hp]
                s = _dot_nt(qs[h], ks)
                mn = jnp.maximum(m, jnp.max(s, axis=-1, keepdims=True))
                p = jnp.exp2(s - mn).astype(BF16)
                new.append((mn, jnp.exp2(m - mn) * acc + _dot(p, vs)))
            return tuple(new)

        init = tuple((jnp.full((tq, 1), NEG_INF, F32), jnp.zeros((tq, LANES), F32)) for _ in range(2))
        res = lax.fori_loop(0, nk, body, init)
        finish([res[0][1], res[1][1]])


def _mla_attention(q, k, v, tq=512, tk=512):
    B, S, _ = q.shape
    tk = min(tk, S)
    return pl.pallas_call(
        functools.partial(_mla_kernel, tq=tq, tk=tk, nk=S // tk),
        grid=(B, MLA_HEADS // 2, S // tq),
        in_specs=[
            pl.BlockSpec((None, tq, 2 * MLA_HEAD_PAD), lambda b, h, i: (b, i, h)),
            pl.BlockSpec((None, S, 2 * MLA_HEAD_PAD), lambda b, h, i: (b, 0, h)),
            pl.BlockSpec((None, S, 2 * MLA_HEAD_PAD), lambda b, h, i: (b, 0, h)),
        ],
        out_specs=pl.BlockSpec((None, tq, 2 * MLA_V), lambda b, h, i: (b, i, h)),
        out_shape=jax.ShapeDtypeStruct((B, S, MLA_HEADS * MLA_V), BF16),
        compiler_params=pltpu.CompilerParams(
            dimension_semantics=("parallel", "parallel", "parallel"), vmem_limit_bytes=VMEM_LIMIT),
        name="mla_attention",
    )(q, k, v)


def _band_kernel(*refs, tq, nsub, L, W, half_w, has_sink, want_lse):
    var_ref = refs[0]
    refs = refs[1:]
    if has_sink:
        sink_ref = refs[0]
        refs = refs[1:]
    q_ref, k_ref, v_ref, bias_ref = refs[:4]
    o_ref = refs[4]
    lse_ref = refs[5] if want_lse else None

    lane = lax.broadcasted_iota(jnp.int32, (tq, LANES), 1)
    low = lane < HEAD_DIM
    lane_w = lax.broadcasted_iota(jnp.int32, (W, LANES), 1)
    low_w = lane_w < HEAD_DIM
    one = jnp.ones((), BF16)

    def run(exact):
        bad = jnp.zeros((), F32)
        for sub in range(nsub):
            g = pl.program_id(1) * nsub + sub
            rows = slice(sub * tq, (sub + 1) * tq)
            if W == L:
                k = k_ref[...]
                v = v_ref[...]
            else:
                start = pl.multiple_of(jnp.clip(g * tq - half_w, 0, L - W), HEAD_DIM)
                k = k_ref[pl.ds(start, W), :]
                v = v_ref[pl.ds(start, W), :]
            q = q_ref[rows, :]
            variant = var_ref[g]
            for pair in range(2):
                sl = slice(pair * LANES, (pair + 1) * LANES)
                qp, kp, vp = q[:, sl], k[:, sl], v[:, sl]
                outs, lses = [], []
                for hh in range(2):
                    head = 2 * pair + hh
                    mine = low if hh == 0 else jnp.logical_not(low)
                    qm = jnp.where(mine, qp, jnp.zeros_like(qp))
                    s = _dot_nt(qm, kp) + bias_ref[variant, head]
                    sk = sink_ref[head] * LOG2E if has_sink else None
                    if exact:
                        m = jnp.max(s, axis=-1, keepdims=True)
                        if has_sink:
                            m = jnp.maximum(m, sk)
                        s = s - m
                        sk = sk - m if has_sink else None
                    p = jnp.exp2(s).astype(BF16)
                    vx = jnp.where(low_w, vp, one) if hh == 0 else jnp.where(low_w, one, vp)
                    acc = _dot(p, vx)
                    if has_sink:
                        acc = acc + jnp.where(mine, 0.0, jnp.exp2(sk))
                    den = pltpu.roll(acc, HEAD_DIM, 1)
                    outs.append(acc / den)
                    if want_lse:
                        lses.append(jnp.log(den) + m * LN2 if exact else jnp.log(den))
                    if not exact:
                        good = jnp.isfinite(acc) & (jnp.where(mine, den, acc) >= 2.0 ** -64)
                        bad = jnp.maximum(bad, jnp.max(jnp.where(good, 0.0, 1.0)))
                o_ref[rows, sl] = jnp.where(low, outs[0], outs[1]).astype(BF16)
                if want_lse:
                    lse_ref[rows, sl] = jnp.where(low, lses[0], lses[1])
        return bad

    bad = run(exact=False)

    @pl.when(bad != 0.0)
    def _():
        run(exact=True)


def _band_bias(tq, L, W, half_w, slopes):
    offs = [int(np.clip(i * tq - half_w, 0, L - W)) - i * tq for i in range(L // tq)]
    uniq = sorted(set(offs))
    var = np.asarray([uniq.index(o) for o in offs], np.int32)
    rel = (np.arange(W)[None, :] - np.arange(tq)[:, None])[None] + np.asarray(uniq)[:, None, None]
    dist = np.abs(rel).astype(np.float64)
    sl = np.asarray(slopes, np.float64)[None, :, None, None]
    bias = np.where(dist[:, None] <= half_w, -sl * dist[:, None] * LOG2E, NEG_INF)
    return jnp.asarray(var), jnp.asarray(bias, F32)


def _band_attention(qkv, half_w, slopes, sink=None, want_lse=True, tq=256):
    N, L, _ = qkv.shape
    tq = min(tq, L)
    W = min(L, tq + 2 * half_w)
    nq = 4 * HEAD_DIM
    var, bias = _band_bias(tq, L, W, half_w, slopes)
    nsub = 2 if L >= 2 * tq else 1
    ts = tq * nsub
    in_specs = [
        pl.BlockSpec((None, ts, nq), lambda n, i, var: (n, i, 0)),
        pl.BlockSpec((None, L, nq), lambda n, i, var: (n, 0, 1)),
        pl.BlockSpec((None, L, nq), lambda n, i, var: (n, 0, 2)),
        pl.BlockSpec(bias.shape, lambda n, i, var: (0, 0, 0, 0), pipeline_mode=pl.Buffered(1)),
    ]
    args = [qkv, qkv, qkv, bias]
    if sink is not None:
        in_specs = [pl.BlockSpec(memory_space=pltpu.SMEM)] + in_specs
        args = [sink] + args
    out_shape = [jax.ShapeDtypeStruct((N, L, nq), BF16)]
    out_specs = [pl.BlockSpec((None, ts, nq), lambda n, i, var: (n, i, 0))]
    if want_lse:
        out_shape.append(jax.ShapeDtypeStruct((N, L, nq), F32))
        out_specs.append(pl.BlockSpec((None, ts, nq), lambda n, i, var: (n, i, 0)))
    outs = pl.pallas_call(
        functools.partial(_band_kernel, tq=tq, nsub=nsub, L=L, W=W, half_w=half_w,
                          has_sink=sink is not None, want_lse=want_lse),
        grid_spec=pltpu.PrefetchScalarGridSpec(
            num_scalar_prefetch=1, grid=(N, L // ts), in_specs=in_specs, out_specs=out_specs),
        out_shape=out_shape,
        compiler_params=pltpu.CompilerParams(
            dimension_semantics=("parallel", "parallel"), vmem_limit_bytes=VMEM_LIMIT),
        name="band_attention",
    )(var, *args)
    return outs


def _out_proj_kernel(x_ref, ya_ref, o1_ref, l1_ref, o2_ref, l2_ref, o3_ref, l3_ref, oc_ref,
                     wo_ref, g2_ref, wr_ref, cnt0_ref, xm_ref, hg_ref, meta_ref, cnt_ref,
                     scr_ref, cnt_scr, *, tm):
    nq = DIL_HEADS * HEAD_DIM
    ncs = nq // LANES
    slab = 0
    merged = []
    for r, o_ref, l_ref in ((DIL_PAIRS[1][1], o2_ref, l2_ref), (DIL_PAIRS[2][1], o3_ref, l3_ref)):
        for j in range(r):
            oj = o_ref[j].astype(F32)
            lj = l_ref[j]
            for c in range(ncs):
                scr_ref[slab + c, pl.ds(j, tm // r, stride=r), :] = oj[:, c * LANES:(c + 1) * LANES]
                scr_ref[slab + ncs + c, pl.ds(j, tm // r, stride=r), :] = lj[:, c * LANES:(c + 1) * LANES]
        on = jnp.concatenate([scr_ref[slab + c] for c in range(ncs)], axis=1)
        ln = jnp.concatenate([scr_ref[slab + ncs + c] for c in range(ncs)], axis=1)
        merged.append((on, ln))
        slab += 2 * ncs
    o1 = o1_ref[...].astype(F32)
    l1 = l1_ref[...]
    (o2, l2), (o3, l3) = merged
    mx = jnp.maximum(l1, jnp.maximum(l2, l3))
    e1, e2, e3 = jnp.exp(l1 - mx), jnp.exp(l2 - mx), jnp.exp(l3 - mx)
    yb = (e1 * o1 + e2 * o2 + e3 * o3) / (e1 + e2 + e3)

    na = MLA_HEADS * MLA_V
    y = (_dot(ya_ref[...], wo_ref[0:na, :])
         + _dot(yb.astype(BF16), wo_ref[na:na + nq, :])
         + _dot(oc_ref[...], wo_ref[na + nq:, :]))
    xm = x_ref[...] + y
    xm_ref[...] = xm
    h2f = _rms(xm, g2_ref[...])
    h2 = h2f.astype(BF16)

    logits = _dot(h2, wr_ref[...])
    lane = lax.broadcasted_iota(jnp.int32, (tm, LANES), 1).astype(F32)
    big = float(LANES)
    lg = jnp.where(lane < MOE_GROUPS, logits, NEG_INF)
    gmax = jnp.max(lg, axis=-1, keepdims=True)
    gsel = jnp.min(jnp.where(lg == gmax, lane, big), axis=-1, keepdims=True)
    gden = jnp.sum(jnp.where(lane < MOE_GROUPS, jnp.exp(lg - gmax), 0.0), axis=-1, keepdims=True)
    gprob = 1.0 / gden
    lo = MOE_GROUPS + MOE_EPG * gsel
    le = jnp.where((lane >= lo) & (lane < lo + MOE_EPG), logits, NEG_INF)
    t1 = jnp.max(le, axis=-1, keepdims=True)
    i1 = jnp.min(jnp.where(le == t1, lane, big), axis=-1, keepdims=True)
    le2 = jnp.where(lane == i1, NEG_INF, le)
    t2 = jnp.max(le2, axis=-1, keepdims=True)
    i2 = jnp.min(jnp.where(le2 == t2, lane, big), axis=-1, keepdims=True)
    ex = jnp.exp(t2 - t1)
    gate1 = gprob / (1.0 + ex)
    gate2 = gprob * ex / (1.0 + ex)
    e1, e2 = i1 - MOE_GROUPS, i2 - MOE_GROUPS
    first = e1 < e2
    la = jnp.minimum(e1, e2) - MOE_EPG * gsel
    lb = jnp.maximum(e1, e2) - MOE_EPG * gsel
    cls = gsel * PAIRS_PER_GROUP + la * (2 * MOE_EPG - 1 - la) * 0.5 + (lb - la - 1.0)
    onehot = lane == cls

    @pl.when((pl.program_id(0) == 0) & (pl.program_id(1) == 0))
    def _():
        cnt_scr[...] = cnt0_ref[...]

    tri = (lax.broadcasted_iota(jnp.int32, (tm, tm), 0) >= lax.broadcasted_iota(jnp.int32, (tm, tm), 1))
    prefix = _dot(tri.astype(BF16), onehot.astype(BF16))
    base = cnt_scr[...]
    rank = jnp.sum(jnp.where(onehot, prefix + base, 0.0), axis=-1, keepdims=True) - 1.0
    cnt_scr[...] = base + prefix[tm - 1:tm, :]
    cnt_ref[...] = cnt_scr[...]
    meta_ref[...] = jnp.where(lane == 0, cls, jnp.where(lane == 1, rank, 0.0))

    for c in range(D_MODEL // LANES):
        hg_ref[pl.ds(c, tm, stride=ROW_CHUNKS), :] = h2f[:, c * LANES:(c + 1) * LANES]
    hg_ref[pl.ds(ROW_CHUNKS - 2, tm, stride=ROW_CHUNKS), :] = jnp.broadcast_to(
        jnp.where(first, gate1, gate2), (tm, LANES))
    hg_ref[pl.ds(ROW_CHUNKS - 1, tm, stride=ROW_CHUNKS), :] = jnp.broadcast_to(
        jnp.where(first, gate2, gate1), (tm, LANES))


def _out_proj(x, ya, b1, b2, b3, oc, cnt0, lw, tm=2 * TOKEN_TILE):
    B, S, _ = x.shape
    nq = DIL_HEADS * HEAD_DIM
    r2, r3 = DIL_PAIRS[1][1], DIL_PAIRS[2][1]
    tok = lambda c: pl.BlockSpec((None, tm, c), lambda b, i: (b, i, 0))
    res = lambda r: pl.BlockSpec((None, r, tm // r, nq), lambda b, i: (b, 0, i, 0))
    cnt_spec = pl.BlockSpec((1, LANES), lambda b, i: (0, 0))
    return pl.pallas_call(
        functools.partial(_out_proj_kernel, tm=tm),
        grid=(B, S // tm),
        in_specs=[tok(D_MODEL), tok(MLA_HEADS * MLA_V), tok(nq), tok(nq), res(r2), res(r2),
                  res(r3), res(r3), tok(nq),
                  _const_spec((D_MODEL, D_MODEL)), _const_spec((1, D_MODEL)),
                  _const_spec((D_MODEL, LANES)), cnt_spec],
        out_specs=[tok(D_MODEL),
                   pl.BlockSpec((tm * ROW_CHUNKS, LANES), lambda b, i: (b * (S // tm) + i, 0)),
                   tok(LANES), cnt_spec],
        out_shape=[jax.ShapeDtypeStruct((B, S, D_MODEL), F32),
                   jax.ShapeDtypeStruct((B * S * ROW_CHUNKS, LANES), F32),
                   jax.ShapeDtypeStruct((B, S, LANES), F32),
                   jax.ShapeDtypeStruct((1, LANES), F32)],
        scratch_shapes=[pltpu.VMEM((4 * (nq // LANES), tm, LANES), F32), pltpu.VMEM((1, LANES), F32)],
        compiler_params=pltpu.CompilerParams(
            dimension_semantics=("arbitrary", "arbitrary"), vmem_limit_bytes=VMEM_LIMIT),
        name="out_proj_router",
    )(x, ya, b1[0], b1[1], b2[0].reshape(B, r2, S // r2, nq), b2[1].reshape(B, r2, S // r2, nq),
      b3[0].reshape(B, r3, S // r3, nq), b3[1].reshape(B, r3, S // r3, nq), oc,
      lw["w_out"], lw["norm2_g"], lw["w_router"], cnt0)


def _row_copy(src_ref, src_row, dst_ref, dst_row, chunks, sem):
    return pltpu.make_async_copy(src_ref.at[pl.ds(src_row * chunks, chunks)],
                                 dst_ref.at[pl.ds(dst_row * chunks, chunks)], sem)


def _dispatch_kernel(pos_ref, h_ref, init_ref, out_ref, sem, *, chunks):
    del init_ref
    n = h_ref.shape[0] // chunks

    def start(r, c):
        _row_copy(h_ref, r, out_ref, pos_ref[0, r], chunks, sem).start()
        return c

    lax.fori_loop(0, n, start, 0, unroll=8)
    pltpu.make_async_copy(h_ref, out_ref.at[pl.ds(0, n * chunks)], sem).wait()


def _dispatch_rows(hg, pos, buf, rows=ROW_STEP, chunks=ROW_CHUNKS):
    T = pos.shape[0]
    return pl.pallas_call(
        functools.partial(_dispatch_kernel, chunks=chunks),
        grid=(T // rows,),
        in_specs=[pl.BlockSpec((None, 1, rows), lambda i: (i, 0, 0), memory_space=pltpu.SMEM),
                  pl.BlockSpec((rows * chunks, LANES), lambda i: (i, 0)),
                  pl.BlockSpec(memory_space=pl.ANY)],
        out_specs=pl.BlockSpec(memory_space=pl.ANY),
        out_shape=jax.ShapeDtypeStruct(buf.shape, buf.dtype),
        scratch_shapes=[pltpu.SemaphoreType.DMA(())],
        input_output_aliases={2: 0},
        compiler_params=pltpu.CompilerParams(
            dimension_semantics=("arbitrary",), vmem_limit_bytes=VMEM_LIMIT),
        name="moe_dispatch",
    )(pos.reshape(T // rows, 1, rows), hg, buf)


def _gather_kernel(pos_ref, y_ref, out_ref, sem, *, chunks):
    n = out_ref.shape[0] // chunks

    def start(r, c):
        _row_copy(y_ref, pos_ref[0, r], out_ref, r, chunks, sem).start()
        return c

    lax.fori_loop(0, n, start, 0, unroll=8)
    pltpu.make_async_copy(y_ref.at[pl.ds(0, n * chunks)], out_ref, sem).wait()


def _gather_rows(y, pos, rows=ROW_STEP, chunks=OUT_CHUNKS):
    T = pos.shape[0]
    return pl.pallas_call(
        functools.partial(_gather_kernel, chunks=chunks),
        grid=(T // rows,),
        in_specs=[pl.BlockSpec((None, 1, rows), lambda i: (i, 0, 0), memory_space=pltpu.SMEM),
                  pl.BlockSpec(memory_space=pl.ANY)],
        out_specs=pl.BlockSpec((rows * chunks, LANES), lambda i: (i, 0)),
        out_shape=jax.ShapeDtypeStruct((T * chunks, LANES), y.dtype),
        scratch_shapes=[pltpu.SemaphoreType.DMA(())],
        compiler_params=pltpu.CompilerParams(
            dimension_semantics=("arbitrary",), vmem_limit_bytes=VMEM_LIMIT),
        name="moe_gather",
    )(pos.reshape(T // rows, 1, rows), y)


def _load_rows(ref, n, chunks, first=0, count=None):
    count = chunks - first if count is None else count
    return jnp.concatenate([ref[pl.ds(first + c, n, stride=chunks), :] for c in range(count)], axis=1)


def _moe_kernel(ea_ref, eb_ref, valid_ref, hs_ref, wgu_a, wd_a, wgu_b, wd_b, y_ref, *, tile):
    i = pl.program_id(0)

    @pl.when(valid_ref[i] == 1)
    def _():
        nh = D_MODEL // LANES
        h = _load_rows(hs_ref, tile, ROW_CHUNKS, 0, nh).astype(BF16)

        def ffn(wgu_ref, wd_ref, g):
            au = _dot(h, wgu_ref[...])
            a, u = au[:, :MOE_HIDDEN], au[:, MOE_HIDDEN:]
            z = (a * jax.nn.sigmoid(a) * u) * jnp.concatenate([g, g], axis=1)
            return _dot(z.astype(BF16), wd_ref[...])

        y = (ffn(wgu_a, wd_a, hs_ref[pl.ds(nh, tile, stride=ROW_CHUNKS), :])
             + ffn(wgu_b, wd_b, hs_ref[pl.ds(nh + 1, tile, stride=ROW_CHUNKS), :]))
        for c in range(OUT_CHUNKS):
            y_ref[pl.ds(c, tile, stride=OUT_CHUNKS), :] = y[:, c * LANES:(c + 1) * LANES]

    @pl.when(valid_ref[i] == 0)
    def _():
        y_ref[...] = jnp.zeros_like(y_ref)


def _moe_experts(hs, ea, eb, valid, lw, tile=MOE_TILE):
    P = hs.shape[0] // ROW_CHUNKS
    row = lambda c: pl.BlockSpec((tile * c, LANES), lambda i, ea, eb, va: (i, 0))
    wgu = lambda which: pl.BlockSpec(
        (None, D_MODEL, 2 * MOE_HIDDEN),
        (lambda i, ea, eb, va: (ea[i], 0, 0)) if which == 0 else (lambda i, ea, eb, va: (eb[i], 0, 0)))
    wd = lambda which: pl.BlockSpec(
        (None, MOE_HIDDEN, D_MODEL),
        (lambda i, ea, eb, va: (ea[i], 0, 0)) if which == 0 else (lambda i, ea, eb, va: (eb[i], 0, 0)))
    return pl.pallas_call(
        functools.partial(_moe_kernel, tile=tile),
        grid_spec=pltpu.PrefetchScalarGridSpec(
            num_scalar_prefetch=3,
            grid=(P // tile,),
            in_specs=[row(ROW_CHUNKS), wgu(0), wd(0), wgu(1), wd(1)],
            out_specs=row(OUT_CHUNKS),
        ),
        out_shape=jax.ShapeDtypeStruct((P * OUT_CHUNKS, LANES), F32),
        compiler_params=pltpu.CompilerParams(
            dimension_semantics=("arbitrary",), vmem_limit_bytes=VMEM_LIMIT),
        name="moe_experts",
    )(ea, eb, valid, hs, lw["w_gu"], lw["w_d"], lw["w_gu"], lw["w_d"])


def _final_kernel(x_ref, y_ref, g_ref, o_ref, *, tm):
    o_ref[...] = _rms(x_ref[...] + _load_rows(y_ref, tm, OUT_CHUNKS), g_ref[...])


def _final_norm(x, y, g, tm=512):
    B, S, _ = x.shape
    tok = pl.BlockSpec((None, tm, D_MODEL), lambda b, i: (b, i, 0))
    yrow = pl.BlockSpec((tm * OUT_CHUNKS, LANES), lambda b, i: (b * (S // tm) + i, 0))
    return pl.pallas_call(
        functools.partial(_final_kernel, tm=tm),
        grid=(B, S // tm),
        in_specs=[tok, yrow, _const_spec((1, D_MODEL))],
        out_specs=tok,
        out_shape=jax.ShapeDtypeStruct((B, S, D_MODEL), F32),
        compiler_params=pltpu.CompilerParams(dimension_semantics=("parallel", "parallel")),
        name="final_norm",
    )(x, y, g)


def _rot_cols(w):
    half = w.shape[1] // 2
    return jnp.concatenate([-w[:, half:], w[:, :half]], axis=1)


def _prep_layer(p, l):
    w_in = p["w_in"][l]
    dm = w_in.shape[0]
    z = lambda n, rows=dm: jnp.zeros((rows, n), F32)
    kr = w_in[:, OFF_KRA:A_COLS]
    pc = w_in[:, A_COLS + B_COLS:]
    nq, nk = SWA_Q_HEADS * HEAD_DIM, SWA_KV_HEADS * HEAD_DIM
    dup = lambda w: jnp.concatenate([w[:, :HEAD_DIM]] * 2 + [w[:, HEAD_DIM:]] * 2, axis=1)
    w_big = jnp.concatenate([
        w_in[:, :OFF_KRA],
        z(MLA_NOPE), kr, z(MLA_HEAD_PAD - MLA_NOPE - MLA_ROPE),
        z(MLA_NOPE), _rot_cols(kr), z(MLA_HEAD_PAD - MLA_NOPE - MLA_ROPE),
        w_in[:, A_COLS:A_COLS + B_COLS],
        pc[:, :nq], dup(pc[:, nq:nq + nk]), dup(pc[:, nq + nk:]),
    ], axis=1).astype(BF16)

    w_uq = p["mla_w_uq"][l].reshape(MLA_Q_LORA, MLA_HEADS, MLA_NOPE + MLA_ROPE)
    pad = MLA_HEAD_PAD - MLA_NOPE - MLA_ROPE
    zq = lambda n: jnp.zeros((MLA_Q_LORA, MLA_HEADS, n), F32)
    pe = w_uq[:, :, MLA_NOPE:]
    pe_rot = jnp.concatenate([-pe[:, :, MLA_ROPE // 2:], pe[:, :, :MLA_ROPE // 2]], axis=2)
    wq_a = jnp.concatenate([w_uq, zq(pad)], axis=2).reshape(MLA_Q_LORA, -1)
    wq_b = jnp.concatenate([zq(MLA_NOPE), pe_rot, zq(pad)], axis=2).reshape(MLA_Q_LORA, -1)
    w_q = jnp.concatenate([wq_a, wq_b], axis=1).astype(BF16)

    w_ukv = p["mla_w_ukv"][l].reshape(MLA_KV_LORA, MLA_HEADS, MLA_NOPE + MLA_V)
    zk = jnp.zeros((MLA_KV_LORA, MLA_HEADS, MLA_HEAD_PAD - MLA_NOPE), F32)
    wk = jnp.concatenate([w_ukv[:, :, :MLA_NOPE], zk], axis=2)
    wv = jnp.concatenate([w_ukv[:, :, MLA_NOPE:], zk], axis=2)
    w_kv = jnp.concatenate([wk.reshape(MLA_KV_LORA, -1), wv.reshape(MLA_KV_LORA, -1)], axis=1).astype(BF16)

    w_router = jnp.concatenate([p["w_router_group"][l], p["w_router_expert"][l],
                                z(LANES - MOE_GROUPS - MOE_EXPERTS)], axis=1).astype(BF16)
    return {
        "norm1_g": p["norm1_g"][l][None, :], "w_big": w_big,
        "q_norm_g": p["mla_q_norm_g"][l][None, :], "w_q": w_q,
        "kv_norm_g": p["mla_kv_norm_g"][l][None, :], "w_kv": w_kv,
        "sink": p["swa_sink"][l].astype(F32),
        "w_out": p["w_out"][l].astype(BF16), "norm2_g": p["norm2_g"][l][None, :],
        "w_router": w_router,
        "w_gu": jnp.concatenate([p["w_gate"][l], p["w_up"][l]], axis=2).astype(BF16),
        "w_d": p["w_down"][l].astype(BF16),
    }


def _rope_tables(S):
    pos = jnp.arange(S, dtype=F32)
    freqs = ROPE_THETA ** (-jnp.arange(0, MLA_ROPE, 2, dtype=F32) / MLA_ROPE)
    ang = pos[:, None] * freqs[None, :]
    cos, sin = jnp.cos(ang), jnp.sin(ang)
    pad = MLA_HEAD_PAD - MLA_NOPE - MLA_ROPE
    cos_t = jnp.concatenate([jnp.ones((S, MLA_NOPE), F32), cos, cos, jnp.zeros((S, pad), F32)], axis=1)
    sin_t = jnp.concatenate([jnp.zeros((S, MLA_NOPE), F32), sin, sin, jnp.zeros((S, pad), F32)], axis=1)
    return cos_t, sin_t


def _pair_tables():
    ea, eb = [], []
    for g in range(MOE_GROUPS):
        for a in range(MOE_EPG):
            for b in range(a + 1, MOE_EPG):
                ea.append(g * MOE_EPG + a)
                eb.append(g * MOE_EPG + b)
    return np.asarray(ea, np.int32), np.asarray(eb, np.int32)


def _moe(hgs, metas, counts, lw, tile=MOE_TILE):
    T = sum(m.shape[0] * m.shape[1] for m in metas)
    P = T + N_PAIR_CLASSES * tile
    cnt = counts[0, :N_PAIR_CLASSES].astype(jnp.int32)
    padded = ((cnt + tile - 1) // tile) * tile
    ends = jnp.cumsum(padded)
    starts = ends - padded
    tile_start = jnp.arange(P // tile, dtype=jnp.int32) * tile
    tcls = jnp.sum((tile_start[:, None] >= ends[None, :]).astype(jnp.int32), axis=1)
    valid = (tile_start < ends[-1]).astype(jnp.int32)
    tcls = jnp.where(valid == 1, tcls, jnp.max(jnp.where(valid == 1, tcls, 0)))
    ta, tb = _pair_tables()
    ea, eb = jnp.asarray(ta)[tcls], jnp.asarray(tb)[tcls]

    poss = []
    buf = jnp.zeros((P * ROW_CHUNKS, LANES), F32)
    classes = jnp.arange(N_PAIR_CLASSES, dtype=jnp.int32)
    for hg, meta in zip(hgs, metas):
        m = meta.reshape(-1, LANES)
        cls = m[:, 0].astype(jnp.int32)
        first = jnp.sum(jnp.where(cls[:, None] == classes[None, :], starts[None, :], 0), axis=1)
        pos = first + m[:, 1].astype(jnp.int32)
        poss.append(pos)
        buf = _dispatch_rows(hg, pos, buf)
    y = _moe_experts(buf, ea, eb, valid, lw)
    return [_gather_rows(y, pos) for pos in poss]


def _mixers(x, y, cnt0, lw, tabs, slopes):
    B, S, _ = x.shape
    x, (q, k, v, g1, g2, g3, gc) = _in_proj(x, y, lw, tabs)
    ya = _mla_attention(q, k, v)
    nb = B_GROUP_COLS
    sl_c, sl_b = slopes[:SWA_Q_HEADS], slopes[SWA_Q_HEADS:].reshape(3, DIL_HEADS)
    outs = []
    for gi, arr in enumerate((g1, g2, g3)):
        window, r = DIL_PAIRS[gi]
        outs.append(_band_attention(arr.reshape(B * r, S // r, nb), window // (2 * r), sl_b[gi] * r))
    oc = _band_attention(gc, SWA_HALF_WINDOW, sl_c, sink=lw["sink"], want_lse=False)[0]
    return _out_proj(x, ya, outs[0], outs[1], outs[2], oc, cnt0, lw)


def kernel(x_prompt, x_sample, norm1_g, w_in, mla_q_norm_g, mla_w_uq, mla_kv_norm_g, mla_w_ukv, swa_sink,
           w_out, norm2_g, w_router_group, w_router_expert, w_gate, w_up, w_down, final_norm_g):
    p = dict(norm1_g=norm1_g, w_in=w_in, mla_q_norm_g=mla_q_norm_g, mla_w_uq=mla_w_uq,
             mla_kv_norm_g=mla_kv_norm_g, mla_w_ukv=mla_w_ukv, swa_sink=swa_sink, w_out=w_out,
             norm2_g=norm2_g, w_router_group=w_router_group, w_router_expert=w_router_expert,
             w_gate=w_gate, w_up=w_up, w_down=w_down)
    slopes = _alibi_slopes()
    xs = [x_prompt, x_sample]
    tabs = [_rope_tables(x.shape[1]) for x in xs]
    ys = [None, None]
    for l in range(DEPTH):
        lw = _prep_layer(p, l)
        xm, hg, meta = [], [], []
        cnt = jnp.zeros((1, LANES), F32)
        for i in range(2):
            a, b, c, cnt = _mixers(xs[i], ys[i], cnt, lw, tabs[i], slopes)
            xm.append(a)
            hg.append(b)
            meta.append(c)
        xs = xm
        ys = _moe(hg, meta, cnt, lw)
    g = final_norm_g[None, :]
    return tuple(_final_norm(xs[i], ys[i], g) for i in range(2))
```

```python
import functools

import numpy as np
import jax
import jax.numpy as jnp
from jax import lax
from jax.experimental import pallas as pl
from jax.experimental.pallas import tpu as pltpu

F32 = jnp.float32
BF16 = jnp.bfloat16

D_MODEL = 1024
DEPTH = 2
HEAD_DIM = 64
MLA_HEADS = 8
MLA_Q_LORA = 384
MLA_KV_LORA = 256
MLA_NOPE = 64
MLA_ROPE = 32
MLA_V = 64
ROPE_THETA = 10000.0
DIL_PAIRS = ((128, 1), (512, 4), (2048, 16))
DIL_HEADS = 4
SWA_Q_HEADS = 4
SWA_KV_HEADS = 2
SWA_HALF_WINDOW = 128
N_ALIBI_HEADS = 16
MOE_GROUPS = 4
MOE_EPG = 8
MOE_EXPERTS = 32
MOE_HIDDEN = 256
RMS_EPS = 1e-6
NEG_INF = -1e30

LANES = 128
MLA_HEAD_PAD = 128
A_COLS = MLA_Q_LORA + MLA_KV_LORA + MLA_ROPE
B_GROUP_COLS = 3 * DIL_HEADS * HEAD_DIM
B_COLS = 3 * B_GROUP_COLS
OFF_CQ = 0
OFF_CKV = MLA_Q_LORA
OFF_KRA = OFF_CKV + MLA_KV_LORA
OFF_G1 = OFF_KRA + LANES
OFF_G2 = OFF_G1 + B_GROUP_COLS
OFF_G3 = OFF_G2 + B_GROUP_COLS
OFF_GC = OFF_G3 + B_GROUP_COLS
BIG_COLS = OFF_GC + B_GROUP_COLS

LOG2E = float(np.log2(np.e))
LN2 = float(np.log(2.0))
BAND_Q_SCALE = float(HEAD_DIM ** -0.5 * np.log2(np.e))
MLA_Q_SCALE = float((MLA_NOPE + MLA_ROPE) ** -0.5 * np.log2(np.e))
PAIRS_PER_GROUP = MOE_EPG * (MOE_EPG - 1) // 2
N_PAIR_CLASSES = MOE_GROUPS * PAIRS_PER_GROUP
ROW_CHUNKS = D_MODEL // LANES + 2
OUT_CHUNKS = D_MODEL // LANES
ROW_STEP = 1024
MOE_TILE = 256
TOKEN_TILE = 256
VMEM_LIMIT = 56 * 1024 * 1024


def _alibi_slopes():
    return 2.0 ** (-8.0 * np.arange(1, N_ALIBI_HEADS + 1, dtype=np.float64) / N_ALIBI_HEADS)


def _rms(x, g):
    return x * lax.rsqrt(jnp.mean(x * x, axis=-1, keepdims=True) + RMS_EPS) * g


def _dot(a, b):
    return jnp.dot(a, b, preferred_element_type=F32)


def _dot_nt(a, b):
    return lax.dot_general(a, b, (((1,), (1,)), ((), ())), preferred_element_type=F32)


def _const_spec(shape):
    nd = len(shape)
    return pl.BlockSpec(shape, lambda *_: (0,) * nd, pipeline_mode=pl.Buffered(1))


def _in_proj_kernel(*refs, has_y, tm):
    if has_y:
        x_ref, y_ref = refs[:2]
        refs = refs[2:]
    else:
        x_ref = refs[0]
        refs = refs[1:]
    (g1_ref, wbig_ref, qg_ref, wq_ref, kvg_ref, wkv_ref, cos_ref, sin_ref) = refs[:8]
    refs = refs[8:]
    if has_y:
        xo_ref = refs[0]
        refs = refs[1:]
    q_ref, k_ref, v_ref, b1_ref, b2_ref, b3_ref, c_ref, scr_ref = refs

    x = x_ref[...]
    if has_y:
        x = x + _load_rows(y_ref, tm, OUT_CHUNKS)
        xo_ref[...] = x
    h = _rms(x, g1_ref[...]).astype(BF16)
    proj = _dot(h, wbig_ref[...])

    cos = cos_ref[...]
    sin = sin_ref[...]
    cos8 = jnp.concatenate([cos] * MLA_HEADS, axis=1)
    sin8 = jnp.concatenate([sin] * MLA_HEADS, axis=1)
    hw = MLA_HEADS * MLA_HEAD_PAD

    hl1 = lax.broadcasted_iota(jnp.int32, (tm, MLA_HEAD_PAD), 1)
    first_half = hl1 < MLA_NOPE + MLA_ROPE // 2

    def rot(b):
        half = MLA_ROPE // 2
        return jnp.where(first_half, -pltpu.roll(b, MLA_HEAD_PAD - half, 1), pltpu.roll(b, half, 1))

    cqn = _rms(proj[:, OFF_CQ:OFF_CKV], qg_ref[...]).astype(BF16)
    qa = _dot(cqn, wq_ref[...])
    qr = jnp.concatenate([rot(qa[:, h * MLA_HEAD_PAD:(h + 1) * MLA_HEAD_PAD]) for h in range(MLA_HEADS)],
                         axis=1)
    q = (qa * cos8 + qr * sin8) * MLA_Q_SCALE
    q_ref[...] = q.astype(BF16)

    ckvn = _rms(proj[:, OFF_CKV:OFF_KRA], kvg_ref[...]).astype(BF16)
    kva = _dot(ckvn, wkv_ref[...])
    kr = proj[:, OFF_KRA:OFF_G1]
    kpe = kr * cos + rot(kr) * sin
    hl = lax.broadcasted_iota(jnp.int32, (tm, hw), 1) % MLA_HEAD_PAD
    k = kva[:, :hw] + jnp.concatenate([kpe] * MLA_HEADS, axis=1) + (hl == MLA_HEAD_PAD - 1).astype(F32)
    k_ref[...] = k.astype(BF16)
    v_ref[...] = (kva[:, hw:] + (hl >= MLA_V).astype(F32)).astype(BF16)

    nslab = B_GROUP_COLS // LANES
    nq_slab = DIL_HEADS * HEAD_DIM // LANES

    def slab(off, c):
        s = proj[:, off + c * LANES: off + (c + 1) * LANES]
        return s * BAND_Q_SCALE if c < nq_slab else s

    b1_ref[...] = jnp.concatenate([slab(OFF_G1, c) for c in range(nslab)], axis=1).astype(BF16)
    c_ref[...] = jnp.concatenate([slab(OFF_GC, c) for c in range(nslab)], axis=1).astype(BF16)

    for off, r, out_ref in ((OFF_G2, DIL_PAIRS[1][1], b2_ref), (OFF_G3, DIL_PAIRS[2][1], b3_ref)):
        for c in range(nslab):
            scr_ref[c] = slab(off, c)
        for j in range(r):
            rows = [scr_ref[c, pl.ds(j, tm // r, stride=r), :] for c in range(nslab)]
            out_ref[j] = jnp.concatenate(rows, axis=1).astype(BF16)


def _in_proj(x, y, lw, tabs, tm=TOKEN_TILE):
    B, S, _ = x.shape
    has_y = y is not None
    hw = MLA_HEADS * MLA_HEAD_PAD
    r2, r3 = DIL_PAIRS[1][1], DIL_PAIRS[2][1]
    tok = lambda c: pl.BlockSpec((None, tm, c), lambda b, i: (b, i, 0))
    in_specs = [tok(D_MODEL)]
    args = [x]
    if has_y:
        in_specs.append(pl.BlockSpec((tm * OUT_CHUNKS, LANES), lambda b, i: (b * (S // tm) + i, 0)))
        args.append(y)
    in_specs += [
        _const_spec((1, D_MODEL)), _const_spec((D_MODEL, BIG_COLS)),
        _const_spec((1, MLA_Q_LORA)), _const_spec((MLA_Q_LORA, hw)),
        _const_spec((1, MLA_KV_LORA)), _const_spec((MLA_KV_LORA, 2 * hw)),
        pl.BlockSpec((tm, LANES), lambda b, i: (i, 0)),
        pl.BlockSpec((tm, LANES), lambda b, i: (i, 0)),
    ]
    args += [lw["norm1_g"], lw["w_big"], lw["q_norm_g"], lw["w_q"], lw["kv_norm_g"], lw["w_kv"],
             tabs[0], tabs[1]]
    out_shape, out_specs = [], []
    if has_y:
        out_shape.append(jax.ShapeDtypeStruct((B, S, D_MODEL), F32))
        out_specs.append(tok(D_MODEL))
    out_shape += [
        jax.ShapeDtypeStruct((B, S, hw), BF16), jax.ShapeDtypeStruct((B, S, hw), BF16),
        jax.ShapeDtypeStruct((B, S, hw), BF16),
        jax.ShapeDtypeStruct((B, S, B_GROUP_COLS), BF16),
        jax.ShapeDtypeStruct((B, r2, S // r2, B_GROUP_COLS), BF16),
        jax.ShapeDtypeStruct((B, r3, S // r3, B_GROUP_COLS), BF16),
        jax.ShapeDtypeStruct((B, S, B_GROUP_COLS), BF16),
    ]
    out_specs += [
        tok(hw), tok(hw), tok(hw), tok(B_GROUP_COLS),
        pl.BlockSpec((None, r2, tm // r2, B_GROUP_COLS), lambda b, i: (b, 0, i, 0)),
        pl.BlockSpec((None, r3, tm // r3, B_GROUP_COLS), lambda b, i: (b, 0, i, 0)),
        tok(B_GROUP_COLS),
    ]
    outs = pl.pallas_call(
        functools.partial(_in_proj_kernel, has_y=has_y, tm=tm),
        grid=(B, S // tm),
        in_specs=in_specs,
        out_specs=out_specs,
        out_shape=out_shape,
        scratch_shapes=[pltpu.VMEM((B_GROUP_COLS // LANES, tm, LANES), F32)],
        compiler_params=pltpu.CompilerParams(
            dimension_semantics=("parallel", "parallel"), vmem_limit_bytes=VMEM_LIMIT),
        name="in_proj",
    )(*args)
    if has_y:
        return outs[0], outs[1:]
    return x, outs


def _mla_kernel(q_ref, k_ref, v_ref, o_ref, *, tq, tk, nk):
    hp = MLA_HEAD_PAD
    lane = lax.broadcasted_iota(jnp.int32, (tq, LANES), 1)
    low = lane < MLA_V
    qs = [q_ref[:, h * hp:(h + 1) * hp] for h in range(2)]

    def finish(accs):
        o0 = accs[0] / pltpu.roll(accs[0], MLA_V, 1)
        o1 = pltpu.roll(accs[1], MLA_V, 1) / accs[1]
        o_ref[...] = jnp.where(low, o0, o1).astype(BF16)

    qx = []
    for h in range(2):
        s0 = _dot_nt(qs[h], k_ref[0:LANES, h * hp:(h + 1) * hp])
        shift = jnp.max(s0, axis=-1, keepdims=True).astype(BF16)
        qx.append(jnp.where(lane == hp - 1, -shift, qs[h]))
    accs = [jnp.zeros((tq, LANES), F32) for _ in range(2)]
    for j in range(nk):
        for h in range(2):
            ks = k_ref[j * tk:(j + 1) * tk, h * hp:(h + 1) * hp]
            vs = v_ref[j * tk:(j + 1) * tk, h * hp:(h + 1) * hp]
            p = jnp.exp2(_dot_nt(qx[h], ks)).astype(BF16)
            accs[h] = accs[h] + _dot(p, vs)
    bad = jnp.max(jnp.where(jnp.isfinite(accs[0]) & jnp.isfinite(accs[1]), 0.0, 1.0))
    finish(accs)

    @pl.when(bad != 0.0)
    def _():
        def body(j, carry):
            start = pl.multiple_of(j * tk, tk)
            new = []
            for h in range(2):
                m, acc = carry[h]
                ks = k_ref[pl.ds(start, tk), h * hp:(h + 1) * hp]
                vs = v_ref[pl.ds(start, tk), h * hp:(h + 1) * hp]
                s = _dot_nt(qs[h], ks)
                mn = jnp.maximum(m, jnp.max(s, axis=-1, keepdims=True))
                p = jnp.exp2(s - mn).astype(BF16)
                new.append((mn, jnp.exp2(m - mn) * acc + _dot(p, vs)))
            return tuple(new)

        init = tuple((jnp.full((tq, 1), NEG_INF, F32), jnp.zeros((tq, LANES), F32)) for _ in range(2))
        res = lax.fori_loop(0, nk, body, init)
        finish([res[0][1], res[1][1]])


def _mla_attention(q, k, v, tq=512, tk=512):
    B, S, _ = q.shape
    tk = min(tk, S)
    return pl.pallas_call(
        functools.partial(_mla_kernel, tq=tq, tk=tk, nk=S // tk),
        grid=(B, MLA_HEADS // 2, S // tq),
        in_specs=[
            pl.BlockSpec((None, tq, 2 * MLA_HEAD_PAD), lambda b, h, i: (b, i, h)),
            pl.BlockSpec((None, S, 2 * MLA_HEAD_PAD), lambda b, h, i: (b, 0, h)),
            pl.BlockSpec((None, S, 2 * MLA_HEAD_PAD), lambda b, h, i: (b, 0, h)),
        ],
        out_specs=pl.BlockSpec((None, tq, 2 * MLA_V), lambda b, h, i: (b, i, h)),
        out_shape=jax.ShapeDtypeStruct((B, S, MLA_HEADS * MLA_V), BF16),
        compiler_params=pltpu.CompilerParams(
            dimension_semantics=("parallel", "parallel", "parallel"), vmem_limit_bytes=VMEM_LIMIT),
        name="mla_attention",
    )(q, k, v)


def _band_kernel(*refs, tq, nsub, L, W, half_w, has_sink, want_lse):
    var_ref = refs[0]
    refs = refs[1:]
    if has_sink:
        sink_ref = refs[0]
        refs = refs[1:]
    q_ref, k_ref, v_ref, bias_ref = refs[:4]
    o_ref = refs[4]
    lse_ref = refs[5] if want_lse else None

    lane = lax.broadcasted_iota(jnp.int32, (tq, LANES), 1)
    low = lane < HEAD_DIM
    lane_w = lax.broadcasted_iota(jnp.int32, (W, LANES), 1)
    low_w = lane_w < HEAD_DIM
    one = jnp.ones((), BF16)

    def run(exact):
        bad = jnp.zeros((), F32)
        for sub in range(nsub):
            g = pl.program_id(1) * nsub + sub
            rows = slice(sub * tq, (sub + 1) * tq)
            if W == L:
                k = k_ref[...]
                v = v_ref[...]
            else:
                start = pl.multiple_of(jnp.clip(g * tq - half_w, 0, L - W), HEAD_DIM)
                k = k_ref[pl.ds(start, W), :]
                v = v_ref[pl.ds(start, W), :]
            q = q_ref[rows, :]
            variant = var_ref[g]
            for pair in range(2):
                sl = slice(pair * LANES, (pair + 1) * LANES)
                qp, kp, vp = q[:, sl], k[:, sl], v[:, sl]
                outs, lses = [], []
                for hh in range(2):
                    head = 2 * pair + hh
                    mine = low if hh == 0 else jnp.logical_not(low)
                    qm = jnp.where(mine, qp, jnp.zeros_like(qp))
                    s = _dot_nt(qm, kp) + bias_ref[variant, head]
                    sk = sink_ref[head] * LOG2E if has_sink else None
                    if exact:
                        m = jnp.max(s, axis=-1, keepdims=True)
                        if has_sink:
                            m = jnp.maximum(m, sk)
                        s = s - m
                        sk = sk - m if has_sink else None
                    p = jnp.exp2(s).astype(BF16)
                    vx = jnp.where(low_w, vp, one) if hh == 0 else jnp.where(low_w, one, vp)
                    acc = _dot(p, vx)
                    if has_sink:
                        acc = acc + jnp.where(mine, 0.0, jnp.exp2(sk))
                    den = pltpu.roll(acc, HEAD_DIM, 1)
                    outs.append(acc / den)
                    if want_lse:
                        lses.append(jnp.log(den) + m * LN2 if exact else jnp.log(den))
                    if not exact:
                        good = jnp.isfinite(acc) & (jnp.where(mine, den, acc) >= 2.0 ** -64)
                        bad = jnp.maximum(bad, jnp.max(jnp.where(good, 0.0, 1.0)))
                o_ref[rows, sl] = jnp.where(low, outs[0], outs[1]).astype(BF16)
                if want_lse:
                    lse_ref[rows, sl] = jnp.where(low, lses[0], lses[1])
        return bad

    bad = run(exact=False)

    @pl.when(bad != 0.0)
    def _():
        run(exact=True)


def _band_bias(tq, L, W, half_w, slopes):
    offs = [int(np.clip(i * tq - half_w, 0, L - W)) - i * tq for i in range(L // tq)]
    uniq = sorted(set(offs))
    var = np.asarray([uniq.index(o) for o in offs], np.int32)
    rel = (np.arange(W)[None, :] - np.arange(tq)[:, None])[None] + np.asarray(uniq)[:, None, None]
    dist = np.abs(rel).astype(np.float64)
    sl = np.asarray(slopes, np.float64)[None, :, None, None]
    bias = np.where(dist[:, None] <= half_w, -sl * dist[:, None] * LOG2E, NEG_INF)
    return jnp.asarray(var), jnp.asarray(bias, F32)


def _band_attention(qkv, half_w, slopes, sink=None, want_lse=True, tq=256):
    N, L, _ = qkv.shape
    tq = min(tq, L)
    W = min(L, tq + 2 * half_w)
    nq = 4 * HEAD_DIM
    var, bias = _band_bias(tq, L, W, half_w, slopes)
    nsub = max(1, min(4, L // tq))
    ts = tq * nsub
    in_specs = [
        pl.BlockSpec((None, ts, nq), lambda n, i, var: (n, i, 0)),
        pl.BlockSpec((None, L, nq), lambda n, i, var: (n, 0, 1)),
        pl.BlockSpec((None, L, nq), lambda n, i, var: (n, 0, 2)),
        pl.BlockSpec(bias.shape, lambda n, i, var: (0, 0, 0, 0), pipeline_mode=pl.Buffered(1)),
    ]
    args = [qkv, qkv, qkv, bias]
    if sink is not None:
        in_specs = [pl.BlockSpec(memory_space=pltpu.SMEM)] + in_specs
        args = [sink] + args
    out_shape = [jax.ShapeDtypeStruct((N, L, nq), BF16)]
    out_specs = [pl.BlockSpec((None, ts, nq), lambda n, i, var: (n, i, 0))]
    if want_lse:
        out_shape.append(jax.ShapeDtypeStruct((N, L, nq), F32))
        out_specs.append(pl.BlockSpec((None, ts, nq), lambda n, i, var: (n, i, 0)))
    outs = pl.pallas_call(
        functools.partial(_band_kernel, tq=tq, nsub=nsub, L=L, W=W, half_w=half_w,
                          has_sink=sink is not None, want_lse=want_lse),
        grid_spec=pltpu.PrefetchScalarGridSpec(
            num_scalar_prefetch=1, grid=(N, L // ts), in_specs=in_specs, out_specs=out_specs),
        out_shape=out_shape,
        compiler_params=pltpu.CompilerParams(
            dimension_semantics=("parallel", "parallel"), vmem_limit_bytes=VMEM_LIMIT),
        name="band_attention",
    )(var, *args)
    return outs


def _out_proj_kernel(x_ref, ya_ref, o1_ref, l1_ref, o2_ref, l2_ref, o3_ref, l3_ref, oc_ref,
                     wo_ref, g2_ref, wr_ref, cnt0_ref, xm_ref, hg_ref, meta_ref, cnt_ref,
                     scr_ref, cnt_scr, *, tm):
    nq = DIL_HEADS * HEAD_DIM
    ncs = nq // LANES
    slab = 0
    merged = []
    for r, o_ref, l_ref in ((DIL_PAIRS[1][1], o2_ref, l2_ref), (DIL_PAIRS[2][1], o3_ref, l3_ref)):
        for j in range(r):
            oj = o_ref[j].astype(F32)
            lj = l_ref[j]
            for c in range(ncs):
                scr_ref[slab + c, pl.ds(j, tm // r, stride=r), :] = oj[:, c * LANES:(c + 1) * LANES]
                scr_ref[slab + ncs + c, pl.ds(j, tm // r, stride=r), :] = lj[:, c * LANES:(c + 1) * LANES]
        on = jnp.concatenate([scr_ref[slab + c] for c in range(ncs)], axis=1)
        ln = jnp.concatenate([scr_ref[slab + ncs + c] for c in range(ncs)], axis=1)
        merged.append((on, ln))
        slab += 2 * ncs
    o1 = o1_ref[...].astype(F32)
    l1 = l1_ref[...]
    (o2, l2), (o3, l3) = merged
    mx = jnp.maximum(l1, jnp.maximum(l2, l3))
    e1, e2, e3 = jnp.exp(l1 - mx), jnp.exp(l2 - mx), jnp.exp(l3 - mx)
    yb = (e1 * o1 + e2 * o2 + e3 * o3) / (e1 + e2 + e3)

    na = MLA_HEADS * MLA_V
    y = (_dot(ya_ref[...], wo_ref[0:na, :])
         + _dot(yb.astype(BF16), wo_ref[na:na + nq, :])
         + _dot(oc_ref[...], wo_ref[na + nq:, :]))
    xm = x_ref[...] + y
    xm_ref[...] = xm
    h2f = _rms(xm, g2_ref[...])
    h2 = h2f.astype(BF16)

    logits = _dot(h2, wr_ref[...])
    lane = lax.broadcasted_iota(jnp.int32, (tm, LANES), 1).astype(F32)
    big = float(LANES)
    lg = jnp.where(lane < MOE_GROUPS, logits, NEG_INF)
    gmax = jnp.max(lg, axis=-1, keepdims=True)
    gsel = jnp.min(jnp.where(lg == gmax, lane, big), axis=-1, keepdims=True)
    gden = jnp.sum(jnp.where(lane < MOE_GROUPS, jnp.exp(lg - gmax), 0.0), axis=-1, keepdims=True)
    gprob = 1.0 / gden
    lo = MOE_GROUPS + MOE_EPG * gsel
    le = jnp.where((lane >= lo) & (lane < lo + MOE_EPG), logits, NEG_INF)
    t1 = jnp.max(le, axis=-1, keepdims=True)
    i1 = jnp.min(jnp.where(le == t1, lane, big), axis=-1, keepdims=True)
    le2 = jnp.where(lane == i1, NEG_INF, le)
    t2 = jnp.max(le2, axis=-1, keepdims=True)
    i2 = jnp.min(jnp.where(le2 == t2, lane, big), axis=-1, keepdims=True)
    ex = jnp.exp(t2 - t1)
    gate1 = gprob / (1.0 + ex)
    gate2 = gprob * ex / (1.0 + ex)
    e1, e2 = i1 - MOE_GROUPS, i2 - MOE_GROUPS
    first = e1 < e2
    la = jnp.minimum(e1, e2) - MOE_EPG * gsel
    lb = jnp.maximum(e1, e2) - MOE_EPG * gsel
    cls = gsel * PAIRS_PER_GROUP + la * (2 * MOE_EPG - 1 - la) * 0.5 + (lb - la - 1.0)
    onehot = lane == cls

    @pl.when((pl.program_id(0) == 0) & (pl.program_id(1) == 0))
    def _():
        cnt_scr[...] = cnt0_ref[...]

    tri = (lax.broadcasted_iota(jnp.int32, (tm, tm), 0) >= lax.broadcasted_iota(jnp.int32, (tm, tm), 1))
    prefix = _dot(tri.astype(BF16), onehot.astype(BF16))
    base = cnt_scr[...]
    rank = jnp.sum(jnp.where(onehot, prefix + base, 0.0), axis=-1, keepdims=True) - 1.0
    cnt_scr[...] = base + prefix[tm - 1:tm, :]
    cnt_ref[...] = cnt_scr[...]
    meta_ref[...] = jnp.where(lane == 0, cls, jnp.where(lane == 1, rank, 0.0))

    for c in range(D_MODEL // LANES):
        hg_ref[pl.ds(c, tm, stride=ROW_CHUNKS), :] = h2f[:, c * LANES:(c + 1) * LANES]
    hg_ref[pl.ds(ROW_CHUNKS - 2, tm, stride=ROW_CHUNKS), :] = jnp.broadcast_to(
        jnp.where(first, gate1, gate2), (tm, LANES))
    hg_ref[pl.ds(ROW_CHUNKS - 1, tm, stride=ROW_CHUNKS), :] = jnp.broadcast_to(
        jnp.where(first, gate2, gate1), (tm, LANES))


def _out_proj(x, ya, b1, b2, b3, oc, cnt0, lw, tm=2 * TOKEN_TILE):
    B, S, _ = x.shape
    nq = DIL_HEADS * HEAD_DIM
    r2, r3 = DIL_PAIRS[1][1], DIL_PAIRS[2][1]
    tok = lambda c: pl.BlockSpec((None, tm, c), lambda b, i: (b, i, 0))
    res = lambda r: pl.BlockSpec((None, r, tm // r, nq), lambda b, i: (b, 0, i, 0))
    cnt_spec = pl.BlockSpec((1, LANES), lambda b, i: (0, 0))
    return pl.pallas_call(
        functools.partial(_out_proj_kernel, tm=tm),
        grid=(B, S // tm),
        in_specs=[tok(D_MODEL), tok(MLA_HEADS * MLA_V), tok(nq), tok(nq), res(r2), res(r2),
                  res(r3), res(r3), tok(nq),
                  _const_spec((D_MODEL, D_MODEL)), _const_spec((1, D_MODEL)),
                  _const_spec((D_MODEL, LANES)), cnt_spec],
        out_specs=[tok(D_MODEL),
                   pl.BlockSpec((tm * ROW_CHUNKS, LANES), lambda b, i: (b * (S // tm) + i, 0)),
                   tok(LANES), cnt_spec],
        out_shape=[jax.ShapeDtypeStruct((B, S, D_MODEL), F32),
                   jax.ShapeDtypeStruct((B * S * ROW_CHUNKS, LANES), F32),
                   jax.ShapeDtypeStruct((B, S, LANES), F32),
                   jax.ShapeDtypeStruct((1, LANES), F32)],
        scratch_shapes=[pltpu.VMEM((4 * (nq // LANES), tm, LANES), F32), pltpu.VMEM((1, LANES), F32)],
        compiler_params=pltpu.CompilerParams(
            dimension_semantics=("arbitrary", "arbitrary"), vmem_limit_bytes=VMEM_LIMIT),
        name="out_proj_router",
    )(x, ya, b1[0], b1[1], b2[0].reshape(B, r2, S // r2, nq), b2[1].reshape(B, r2, S // r2, nq),
      b3[0].reshape(B, r3, S // r3, nq), b3[1].reshape(B, r3, S // r3, nq), oc,
      lw["w_out"], lw["norm2_g"], lw["w_router"], cnt0)


def _row_copy(src_ref, src_row, dst_ref, dst_row, chunks, sem):
    return pltpu.make_async_copy(src_ref.at[pl.ds(src_row * chunks, chunks)],
                                 dst_ref.at[pl.ds(dst_row * chunks, chunks)], sem)


def _dispatch_kernel(pos_ref, h_ref, init_ref, out_ref, sem, *, chunks):
    del init_ref
    n = h_ref.shape[0] // chunks

    def start(i, c):
        for prio in range(2):
            r = 2 * i + prio
            _row_copy(h_ref, r, out_ref, pos_ref[0, r], chunks, sem).start(priority=prio)
        return c

    lax.fori_loop(0, n // 2, start, 0, unroll=4)
    pltpu.make_async_copy(h_ref, out_ref.at[pl.ds(0, n * chunks)], sem).wait()


def _dispatch_rows(hg, pos, buf, rows=ROW_STEP, chunks=ROW_CHUNKS):
    T = pos.shape[0]
    return pl.pallas_call(
        functools.partial(_dispatch_kernel, chunks=chunks),
        grid=(T // rows,),
        in_specs=[pl.BlockSpec((None, 1, rows), lambda i: (i, 0, 0), memory_space=pltpu.SMEM),
                  pl.BlockSpec((rows * chunks, LANES), lambda i: (i, 0)),
                  pl.BlockSpec(memory_space=pl.ANY)],
        out_specs=pl.BlockSpec(memory_space=pl.ANY),
        out_shape=jax.ShapeDtypeStruct(buf.shape, buf.dtype),
        scratch_shapes=[pltpu.SemaphoreType.DMA(())],
        input_output_aliases={2: 0},
        compiler_params=pltpu.CompilerParams(
            dimension_semantics=("arbitrary",), vmem_limit_bytes=VMEM_LIMIT),
        name="moe_dispatch",
    )(pos.reshape(T // rows, 1, rows), hg, buf)


def _gather_kernel(pos_ref, y_ref, out_ref, sem, *, chunks):
    n = out_ref.shape[0] // chunks

    def start(i, c):
        for prio in range(2):
            r = 2 * i + prio
            _row_copy(y_ref, pos_ref[0, r], out_ref, r, chunks, sem).start(priority=prio)
        return c

    lax.fori_loop(0, n // 2, start, 0, unroll=4)
    pltpu.make_async_copy(y_ref.at[pl.ds(0, n * chunks)], out_ref, sem).wait()


def _gather_rows(y, pos, rows=ROW_STEP, chunks=OUT_CHUNKS):
    T = pos.shape[0]
    return pl.pallas_call(
        functools.partial(_gather_kernel, chunks=chunks),
        grid=(T // rows,),
        in_specs=[pl.BlockSpec((None, 1, rows), lambda i: (i, 0, 0), memory_space=pltpu.SMEM),
                  pl.BlockSpec(memory_space=pl.ANY)],
        out_specs=pl.BlockSpec((rows * chunks, LANES), lambda i: (i, 0)),
        out_shape=jax.ShapeDtypeStruct((T * chunks, LANES), y.dtype),
        scratch_shapes=[pltpu.SemaphoreType.DMA(())],
        compiler_params=pltpu.CompilerParams(
            dimension_semantics=("arbitrary",), vmem_limit_bytes=VMEM_LIMIT),
        name="moe_gather",
    )(pos.reshape(T // rows, 1, rows), y)


def _load_rows(ref, n, chunks, first=0, count=None):
    count = chunks - first if count is None else count
    return jnp.concatenate([ref[pl.ds(first + c, n, stride=chunks), :] for c in range(count)], axis=1)


def _moe_kernel(ea_ref, eb_ref, valid_ref, hs_ref, wgu_a, wd_a, wgu_b, wd_b, y_ref, *, tile):
    i = pl.program_id(0)

    @pl.when(valid_ref[i] == 1)
    def _():
        nh = D_MODEL // LANES
        h = _load_rows(hs_ref, tile, ROW_CHUNKS, 0, nh).astype(BF16)

        def ffn(wgu_ref, wd_ref, g):
            au = _dot(h, wgu_ref[...])
            a, u = au[:, :MOE_HIDDEN], au[:, MOE_HIDDEN:]
            z = (a * jax.nn.sigmoid(a) * u) * jnp.concatenate([g, g], axis=1)
            return _dot(z.astype(BF16), wd_ref[...])

        y = (ffn(wgu_a, wd_a, hs_ref[pl.ds(nh, tile, stride=ROW_CHUNKS), :])
             + ffn(wgu_b, wd_b, hs_ref[pl.ds(nh + 1, tile, stride=ROW_CHUNKS), :]))
        for c in range(OUT_CHUNKS):
            y_ref[pl.ds(c, tile, stride=OUT_CHUNKS), :] = y[:, c * LANES:(c + 1) * LANES]

    @pl.when(valid_ref[i] == 0)
    def _():
        y_ref[...] = jnp.zeros_like(y_ref)


def _moe_experts(hs, ea, eb, valid, lw, tile=MOE_TILE):
    P = hs.shape[0] // ROW_CHUNKS
    row = lambda c: pl.BlockSpec((tile * c, LANES), lambda i, ea, eb, va: (i, 0))
    wgu = lambda which: pl.BlockSpec(
        (None, D_MODEL, 2 * MOE_HIDDEN),
        (lambda i, ea, eb, va: (ea[i], 0, 0)) if which == 0 else (lambda i, ea, eb, va: (eb[i], 0, 0)))
    wd = lambda which: pl.BlockSpec(
        (None, MOE_HIDDEN, D_MODEL),
        (lambda i, ea, eb, va: (ea[i], 0, 0)) if which == 0 else (lambda i, ea, eb, va: (eb[i], 0, 0)))
    return pl.pallas_call(
        functools.partial(_moe_kernel, tile=tile),
        grid_spec=pltpu.PrefetchScalarGridSpec(
            num_scalar_prefetch=3,
            grid=(P // tile,),
            in_specs=[row(ROW_CHUNKS), wgu(0), wd(0), wgu(1), wd(1)],
            out_specs=row(OUT_CHUNKS),
        ),
        out_shape=jax.ShapeDtypeStruct((P * OUT_CHUNKS, LANES), F32),
        compiler_params=pltpu.CompilerParams(
            dimension_semantics=("arbitrary",), vmem_limit_bytes=VMEM_LIMIT),
        name="moe_experts",
    )(ea, eb, valid, hs, lw["w_gu"], lw["w_d"], lw["w_gu"], lw["w_d"])


def _final_kernel(x_ref, y_ref, g_ref, o_ref, *, tm):
    o_ref[...] = _rms(x_ref[...] + _load_rows(y_ref, tm, OUT_CHUNKS), g_ref[...])


def _final_norm(x, y, g, tm=512):
    B, S, _ = x.shape
    tok = pl.BlockSpec((None, tm, D_MODEL), lambda b, i: (b, i, 0))
    yrow = pl.BlockSpec((tm * OUT_CHUNKS, LANES), lambda b, i: (b * (S // tm) + i, 0))
    return pl.pallas_call(
        functools.partial(_final_kernel, tm=tm),
        grid=(B, S // tm),
        in_specs=[tok, yrow, _const_spec((1, D_MODEL))],
        out_specs=tok,
        out_shape=jax.ShapeDtypeStruct((B, S, D_MODEL), F32),
        compiler_params=pltpu.CompilerParams(dimension_semantics=("parallel", "parallel")),
        name="final_norm",
    )(x, y, g)


def _prep_layer(p, l):
    w_in = p["w_in"][l]
    dm = w_in.shape[0]
    z = lambda n, rows=dm: jnp.zeros((rows, n), F32)
    kr = w_in[:, OFF_KRA:A_COLS]
    pc = w_in[:, A_COLS + B_COLS:]
    nq, nk = SWA_Q_HEADS * HEAD_DIM, SWA_KV_HEADS * HEAD_DIM
    dup = lambda w: jnp.concatenate([w[:, :HEAD_DIM]] * 2 + [w[:, HEAD_DIM:]] * 2, axis=1)
    w_big = jnp.concatenate([
        w_in[:, :OFF_KRA],
        z(MLA_NOPE), kr, z(MLA_HEAD_PAD - MLA_NOPE - MLA_ROPE),
        w_in[:, A_COLS:A_COLS + B_COLS],
        pc[:, :nq], dup(pc[:, nq:nq + nk]), dup(pc[:, nq + nk:]),
    ], axis=1).astype(BF16)

    w_uq = p["mla_w_uq"][l].reshape(MLA_Q_LORA, MLA_HEADS, MLA_NOPE + MLA_ROPE)
    pad = MLA_HEAD_PAD - MLA_NOPE - MLA_ROPE
    zq = lambda n: jnp.zeros((MLA_Q_LORA, MLA_HEADS, n), F32)
    w_q = jnp.concatenate([w_uq, zq(pad)], axis=2).reshape(MLA_Q_LORA, -1).astype(BF16)

    w_ukv = p["mla_w_ukv"][l].reshape(MLA_KV_LORA, MLA_HEADS, MLA_NOPE + MLA_V)
    zk = jnp.zeros((MLA_KV_LORA, MLA_HEADS, MLA_HEAD_PAD - MLA_NOPE), F32)
    wk = jnp.concatenate([w_ukv[:, :, :MLA_NOPE], zk], axis=2)
    wv = jnp.concatenate([w_ukv[:, :, MLA_NOPE:], zk], axis=2)
    w_kv = jnp.concatenate([wk.reshape(MLA_KV_LORA, -1), wv.reshape(MLA_KV_LORA, -1)], axis=1).astype(BF16)

    w_router = jnp.concatenate([p["w_router_group"][l], p["w_router_expert"][l],
                                z(LANES - MOE_GROUPS - MOE_EXPERTS)], axis=1).astype(BF16)
    return {
        "norm1_g": p["norm1_g"][l][None, :], "w_big": w_big,
        "q_norm_g": p["mla_q_norm_g"][l][None, :], "w_q": w_q,
        "kv_norm_g": p["mla_kv_norm_g"][l][None, :], "w_kv": w_kv,
        "sink": p["swa_sink"][l].astype(F32),
        "w_out": p["w_out"][l].astype(BF16), "norm2_g": p["norm2_g"][l][None, :],
        "w_router": w_router,
        "w_gu": jnp.concatenate([p["w_gate"][l], p["w_up"][l]], axis=2).astype(BF16),
        "w_d": p["w_down"][l].astype(BF16),
    }


def _rope_tables(S):
    pos = jnp.arange(S, dtype=F32)
    freqs = ROPE_THETA ** (-jnp.arange(0, MLA_ROPE, 2, dtype=F32) / MLA_ROPE)
    ang = pos[:, None] * freqs[None, :]
    cos, sin = jnp.cos(ang), jnp.sin(ang)
    pad = MLA_HEAD_PAD - MLA_NOPE - MLA_ROPE
    cos_t = jnp.concatenate([jnp.ones((S, MLA_NOPE), F32), cos, cos, jnp.zeros((S, pad), F32)], axis=1)
    sin_t = jnp.concatenate([jnp.zeros((S, MLA_NOPE), F32), sin, sin, jnp.zeros((S, pad), F32)], axis=1)
    return cos_t, sin_t


def _pair_tables():
    ea, eb = [], []
    for g in range(MOE_GROUPS):
        for a in range(MOE_EPG):
            for b in range(a + 1, MOE_EPG):
                ea.append(g * MOE_EPG + a)
                eb.append(g * MOE_EPG + b)
    return np.asarray(ea, np.int32), np.asarray(eb, np.int32)


def _moe(hgs, metas, counts, lw, tile=MOE_TILE):
    T = sum(m.shape[0] * m.shape[1] for m in metas)
    P = T + N_PAIR_CLASSES * tile
    cnt = counts[0, :N_PAIR_CLASSES].astype(jnp.int32)
    padded = ((cnt + tile - 1) // tile) * tile
    ends = jnp.cumsum(padded)
    starts = ends - padded
    tile_start = jnp.arange(P // tile, dtype=jnp.int32) * tile
    tcls = jnp.sum((tile_start[:, None] >= ends[None, :]).astype(jnp.int32), axis=1)
    valid = (tile_start < ends[-1]).astype(jnp.int32)
    tcls = jnp.where(valid == 1, tcls, jnp.max(jnp.where(valid == 1, tcls, 0)))
    ta, tb = _pair_tables()
    ea, eb = jnp.asarray(ta)[tcls], jnp.asarray(tb)[tcls]

    poss = []
    buf = jnp.zeros((P * ROW_CHUNKS, LANES), F32)
    classes = jnp.arange(N_PAIR_CLASSES, dtype=jnp.int32)
    for hg, meta in zip(hgs, metas):
        m = meta.reshape(-1, LANES)
        cls = m[:, 0].astype(jnp.int32)
        first = jnp.sum(jnp.where(cls[:, None] == classes[None, :], starts[None, :], 0), axis=1)
        pos = first + m[:, 1].astype(jnp.int32)
        poss.append(pos)
        buf = _dispatch_rows(hg, pos, buf)
    y = _moe_experts(buf, ea, eb, valid, lw)
    return [_gather_rows(y, pos) for pos in poss]


def _mixers(x, y, cnt0, lw, tabs, slopes):
    B, S, _ = x.shape
    x, (q, k, v, g1, g2, g3, gc) = _in_proj(x, y, lw, tabs)
    ya = _mla_attention(q, k, v)
    nb = B_GROUP_COLS
    sl_c, sl_b = slopes[:SWA_Q_HEADS], slopes[SWA_Q_HEADS:].reshape(3, DIL_HEADS)
    outs = []
    for gi, arr in enumerate((g1, g2, g3)):
        window, r = DIL_PAIRS[gi]
        outs.append(_band_attention(arr.reshape(B * r, S // r, nb), window // (2 * r), sl_b[gi] * r))
    oc = _band_attention(gc, SWA_HALF_WINDOW, sl_c, sink=lw["sink"], want_lse=False)[0]
    return _out_proj(x, ya, outs[0], outs[1], outs[2], oc, cnt0, lw)


def kernel(x_prompt, x_sample, norm1_g, w_in, mla_q_norm_g, mla_w_uq, mla_kv_norm_g, mla_w_ukv, swa_sink,
           w_out, norm2_g, w_router_group, w_router_expert, w_gate, w_up, w_down, final_norm_g):
    p = dict(norm1_g=norm1_g, w_in=w_in, mla_q_norm_g=mla_q_norm_g, mla_w_uq=mla_w_uq,
             mla_kv_norm_g=mla_kv_norm_g, mla_w_ukv=mla_w_ukv, swa_sink=swa_sink, w_out=w_out,
             norm2_g=norm2_g, w_router_group=w_router_group, w_router_expert=w_router_expert,
             w_gate=w_gate, w_up=w_up, w_down=w_down)
    slopes = _alibi_slopes()
    xs = [x_prompt, x_sample]
    tabs = [_rope_tables(x.shape[1]) for x in xs]
    ys = [None, None]
    for l in range(DEPTH):
        lw = _prep_layer(p, l)
        xm, hg, meta = [], [], []
        cnt = jnp.zeros((1, LANES), F32)
        for i in range(2):
            a, b, c, cnt = _mixers(xs[i], ys[i], cnt, lw, tabs[i], slopes)
            xm.append(a)
            hg.append(b)
            meta.append(c)
        xs = xm
        ys = _moe(hg, meta, cnt, lw)
    g = final_norm_g[None, :]
    return tuple(_final_norm(xs[i], ys[i], g) for i in range(2))
```

```python
import functools

import numpy as np
import jax
import jax.numpy as jnp
from jax import lax
from jax.experimental import pallas as pl
from jax.experimental.pallas import tpu as pltpu

F32 = jnp.float32
BF16 = jnp.bfloat16

D_MODEL = 1024
DEPTH = 2
HEAD_DIM = 64
MLA_HEADS = 8
MLA_Q_LORA = 384
MLA_KV_LORA = 256
MLA_NOPE = 64
MLA_ROPE = 32
MLA_V = 64
ROPE_THETA = 10000.0
DIL_PAIRS = ((128, 1), (512, 4), (2048, 16))
DIL_HEADS = 4
SWA_Q_HEADS = 4
SWA_KV_HEADS = 2
SWA_HALF_WINDOW = 128
N_ALIBI_HEADS = 16
MOE_GROUPS = 4
MOE_EPG = 8
MOE_EXPERTS = 32
MOE_HIDDEN = 256
RMS_EPS = 1e-6
NEG_INF = -1e30

LANES = 128
MLA_HEAD_PAD = 128
A_COLS = MLA_Q_LORA + MLA_KV_LORA + MLA_ROPE
B_GROUP_COLS = 3 * DIL_HEADS * HEAD_DIM
B_COLS = 3 * B_GROUP_COLS
OFF_CQ = 0
OFF_CKV = MLA_Q_LORA
OFF_KRA = OFF_CKV + MLA_KV_LORA
OFF_G1 = OFF_KRA + LANES
OFF_G2 = OFF_G1 + B_GROUP_COLS
OFF_G3 = OFF_G2 + B_GROUP_COLS
OFF_GC = OFF_G3 + B_GROUP_COLS
BIG_COLS = OFF_GC + B_GROUP_COLS

LOG2E = float(np.log2(np.e))
LN2 = float(np.log(2.0))
BAND_Q_SCALE = float(HEAD_DIM ** -0.5 * np.log2(np.e))
MLA_Q_SCALE = float((MLA_NOPE + MLA_ROPE) ** -0.5 * np.log2(np.e))
PAIRS_PER_GROUP = MOE_EPG * (MOE_EPG - 1) // 2
N_PAIR_CLASSES = MOE_GROUPS * PAIRS_PER_GROUP
ROW_CHUNKS = D_MODEL // LANES + 2
OUT_CHUNKS = D_MODEL // LANES
ROW_STEP = 1024
MOE_TILE = 512
TOKEN_TILE = 256
VMEM_LIMIT = 56 * 1024 * 1024


def _alibi_slopes():
    return 2.0 ** (-8.0 * np.arange(1, N_ALIBI_HEADS + 1, dtype=np.float64) / N_ALIBI_HEADS)


def _rms(x, g):
    return x * lax.rsqrt(jnp.mean(x * x, axis=-1, keepdims=True) + RMS_EPS) * g


def _dot(a, b):
    return jnp.dot(a, b, preferred_element_type=F32)


def _dot_nt(a, b):
    return lax.dot_general(a, b, (((1,), (1,)), ((), ())), preferred_element_type=F32)


def _const_spec(shape):
    nd = len(shape)
    return pl.BlockSpec(shape, lambda *_: (0,) * nd, pipeline_mode=pl.Buffered(1))


def _in_proj_kernel(*refs, has_y, tm):
    if has_y:
        x_ref, y_ref = refs[:2]
        refs = refs[2:]
    else:
        x_ref = refs[0]
        refs = refs[1:]
    (g1_ref, wbig_ref, qg_ref, wq_ref, kvg_ref, wkv_ref, cos_ref, sin_ref) = refs[:8]
    refs = refs[8:]
    if has_y:
        xo_ref = refs[0]
        refs = refs[1:]
    q_ref, k_ref, v_ref, b1_ref, b2_ref, b3_ref, c_ref, scr_ref = refs

    x = x_ref[...]
    if has_y:
        x = x + _load_rows(y_ref, tm, OUT_CHUNKS)
        xo_ref[...] = x
    h = _rms(x, g1_ref[...]).astype(BF16)
    proj = _dot(h, wbig_ref[...])

    cos = cos_ref[...]
    sin = sin_ref[...]
    cos8 = jnp.concatenate([cos] * MLA_HEADS, axis=1)
    sin8 = jnp.concatenate([sin] * MLA_HEADS, axis=1)
    hw = MLA_HEADS * MLA_HEAD_PAD

    hl1 = lax.broadcasted_iota(jnp.int32, (tm, MLA_HEAD_PAD), 1)
    first_half = hl1 < MLA_NOPE + MLA_ROPE // 2

    def rot(b):
        half = MLA_ROPE // 2
        return jnp.where(first_half, -pltpu.roll(b, MLA_HEAD_PAD - half, 1), pltpu.roll(b, half, 1))

    cqn = _rms(proj[:, OFF_CQ:OFF_CKV], qg_ref[...]).astype(BF16)
    qa = _dot(cqn, wq_ref[...])
    qr = jnp.concatenate([rot(qa[:, h * MLA_HEAD_PAD:(h + 1) * MLA_HEAD_PAD]) for h in range(MLA_HEADS)],
                         axis=1)
    q = (qa * cos8 + qr * sin8) * MLA_Q_SCALE
    q_ref[...] = q.astype(BF16)

    ckvn = _rms(proj[:, OFF_CKV:OFF_KRA], kvg_ref[...]).astype(BF16)
    kva = _dot(ckvn, wkv_ref[...])
    kr = proj[:, OFF_KRA:OFF_G1]
    kpe = kr * cos + rot(kr) * sin
    hl = lax.broadcasted_iota(jnp.int32, (tm, hw), 1) % MLA_HEAD_PAD
    k = kva[:, :hw] + jnp.concatenate([kpe] * MLA_HEADS, axis=1) + (hl == MLA_HEAD_PAD - 1).astype(F32)
    k_ref[...] = k.astype(BF16)
    v_ref[...] = (kva[:, hw:] + (hl >= MLA_V).astype(F32)).astype(BF16)

    nslab = B_GROUP_COLS // LANES
    nq_slab = DIL_HEADS * HEAD_DIM // LANES

    def slab(off, c):
        s = proj[:, off + c * LANES: off + (c + 1) * LANES]
        return s * BAND_Q_SCALE if c < nq_slab else s

    b1_ref[...] = jnp.concatenate([slab(OFF_G1, c) for c in range(nslab)], axis=1).astype(BF16)
    c_ref[...] = jnp.concatenate([slab(OFF_GC, c) for c in range(nslab)], axis=1).astype(BF16)

    for off, r, out_ref in ((OFF_G2, DIL_PAIRS[1][1], b2_ref), (OFF_G3, DIL_PAIRS[2][1], b3_ref)):
        for c in range(nslab):
            scr_ref[c] = slab(off, c)
        for j in range(r):
            rows = [scr_ref[c, pl.ds(j, tm // r, stride=r), :] for c in range(nslab)]
            out_ref[j] = jnp.concatenate(rows, axis=1).astype(BF16)


def _in_proj(x, y, lw, tabs, tm=TOKEN_TILE):
    B, S, _ = x.shape
    has_y = y is not None
    hw = MLA_HEADS * MLA_HEAD_PAD
    r2, r3 = DIL_PAIRS[1][1], DIL_PAIRS[2][1]
    tok = lambda c: pl.BlockSpec((None, tm, c), lambda b, i: (b, i, 0))
    in_specs = [tok(D_MODEL)]
    args = [x]
    if has_y:
        in_specs.append(pl.BlockSpec((tm * OUT_CHUNKS, LANES), lambda b, i: (b * (S // tm) + i, 0)))
        args.append(y)
    in_specs += [
        _const_spec((1, D_MODEL)), _const_spec((D_MODEL, BIG_COLS)),
        _const_spec((1, MLA_Q_LORA)), _const_spec((MLA_Q_LORA, hw)),
        _const_spec((1, MLA_KV_LORA)), _const_spec((MLA_KV_LORA, 2 * hw)),
        pl.BlockSpec((tm, LANES), lambda b, i: (i, 0)),
        pl.BlockSpec((tm, LANES), lambda b, i: (i, 0)),
    ]
    args += [lw["norm1_g"], lw["w_big"], lw["q_norm_g"], lw["w_q"], lw["kv_norm_g"], lw["w_kv"],
             tabs[0], tabs[1]]
    out_shape, out_specs = [], []
    if has_y:
        out_shape.append(jax.ShapeDtypeStruct((B, S, D_MODEL), F32))
        out_specs.append(tok(D_MODEL))
    out_shape += [
        jax.ShapeDtypeStruct((B, S, hw), BF16), jax.ShapeDtypeStruct((B, S, hw), BF16),
        jax.ShapeDtypeStruct((B, S, hw), BF16),
        jax.ShapeDtypeStruct((B, S, B_GROUP_COLS), BF16),
        jax.ShapeDtypeStruct((B, r2, S // r2, B_GROUP_COLS), BF16),
        jax.ShapeDtypeStruct((B, r3, S // r3, B_GROUP_COLS), BF16),
        jax.ShapeDtypeStruct((B, S, B_GROUP_COLS), BF16),
    ]
    out_specs += [
        tok(hw), tok(hw), tok(hw), tok(B_GROUP_COLS),
        pl.BlockSpec((None, r2, tm // r2, B_GROUP_COLS), lambda b, i: (b, 0, i, 0)),
        pl.BlockSpec((None, r3, tm // r3, B_GROUP_COLS), lambda b, i: (b, 0, i, 0)),
        tok(B_GROUP_COLS),
    ]
    outs = pl.pallas_call(
        functools.partial(_in_proj_kernel, has_y=has_y, tm=tm),
        grid=(B, S // tm),
        in_specs=in_specs,
        out_specs=out_specs,
        out_shape=out_shape,
        scratch_shapes=[pltpu.VMEM((B_GROUP_COLS // LANES, tm, LANES), F32)],
        compiler_params=pltpu.CompilerParams(
            dimension_semantics=("parallel", "parallel"), vmem_limit_bytes=VMEM_LIMIT),
        name="in_proj",
    )(*args)
    if has_y:
        return outs[0], outs[1:]
    return x, outs


def _mla_kernel(q_ref, k_ref, v_ref, o_ref, *, tq, tk, nk):
    hp = MLA_HEAD_PAD
    lane = lax.broadcasted_iota(jnp.int32, (tq, LANES), 1)
    low = lane < MLA_V
    qs = [q_ref[:, h * hp:(h + 1) * hp] for h in range(2)]

    def finish(accs):
        o0 = accs[0] / pltpu.roll(accs[0], MLA_V, 1)
        o1 = pltpu.roll(accs[1], MLA_V, 1) / accs[1]
        o_ref[...] = jnp.where(low, o0, o1).astype(BF16)

    qx = []
    for h in range(2):
        s0 = _dot_nt(qs[h], k_ref[0:LANES, h * hp:(h + 1) * hp])
        shift = jnp.max(s0, axis=-1, keepdims=True).astype(BF16)
        qx.append(jnp.where(lane == hp - 1, -shift, qs[h]))
    accs = [jnp.zeros((tq, LANES), F32) for _ in range(2)]
    for j in range(nk):
        for h in range(2):
            ks = k_ref[j * tk:(j + 1) * tk, h * hp:(h + 1) * hp]
            vs = v_ref[j * tk:(j + 1) * tk, h * hp:(h + 1) * hp]
            p = jnp.exp2(_dot_nt(qx[h], ks)).astype(BF16)
            accs[h] = accs[h] + _dot(p, vs)
    bad = jnp.max(jnp.where(jnp.isfinite(accs[0]) & jnp.isfinite(accs[1]), 0.0, 1.0))
    finish(accs)

    @pl.when(bad != 0.0)
    def _():
        def body(j, carry):
            start = pl.multiple_of(j * tk, tk)
            new = []
            for h in range(2):
                m, acc = carry[h]
                ks = k_ref[pl.ds(start, tk), h * hp:(h + 1) * hp]
                vs = v_ref[pl.ds(start, tk), h * hp:(h + 1) * hp]
                s = _dot_nt(qs[h], ks)
                mn = jnp.maximum(m, jnp.max(s, axis=-1, keepdims=True))
                p = jnp.exp2(s - mn).astype(BF16)
                new.append((mn, jnp.exp2(m - mn) * acc + _dot(p, vs)))
            return tuple(new)

        init = tuple((jnp.full((tq, 1), NEG_INF, F32), jnp.zeros((tq, LANES), F32)) for _ in range(2))
        res = lax.fori_loop(0, nk, body, init)
        finish([res[0][1], res[1][1]])


def _mla_attention(q, k, v, tq=1024, tk=512):
    B, S, _ = q.shape
    tk = min(tk, S)
    return pl.pallas_call(
        functools.partial(_mla_kernel, tq=tq, tk=tk, nk=S // tk),
        grid=(B, MLA_HEADS // 2, S // tq),
        in_specs=[
            pl.BlockSpec((None, tq, 2 * MLA_HEAD_PAD), lambda b, h, i: (b, i, h)),
            pl.BlockSpec((None, S, 2 * MLA_HEAD_PAD), lambda b, h, i: (b, 0, h)),
            pl.BlockSpec((None, S, 2 * MLA_HEAD_PAD), lambda b, h, i: (b, 0, h)),
        ],
        out_specs=pl.BlockSpec((None, tq, 2 * MLA_V), lambda b, h, i: (b, i, h)),
        out_shape=jax.ShapeDtypeStruct((B, S, MLA_HEADS * MLA_V), BF16),
        compiler_params=pltpu.CompilerParams(
            dimension_semantics=("parallel", "parallel", "parallel"), vmem_limit_bytes=VMEM_LIMIT),
        name="mla_attention",
    )(q, k, v)


def _band_kernel(*refs, tq, nsub, L, W, half_w, has_sink, want_lse):
    var_ref = refs[0]
    refs = refs[1:]
    if has_sink:
        sink_ref = refs[0]
        refs = refs[1:]
    q_ref, k_ref, v_ref, bias_ref = refs[:4]
    o_ref = refs[4]
    lse_ref = refs[5] if want_lse else None

    lane = lax.broadcasted_iota(jnp.int32, (tq, LANES), 1)
    low = lane < HEAD_DIM
    lane_w = lax.broadcasted_iota(jnp.int32, (W, LANES), 1)
    low_w = lane_w < HEAD_DIM
    one = jnp.ones((), BF16)

    def run(exact):
        bad = jnp.zeros((), F32)
        for sub in range(nsub):
            g = pl.program_id(1) * nsub + sub
            rows = slice(sub * tq, (sub + 1) * tq)
            if W == L:
                k = k_ref[...]
                v = v_ref[...]
            else:
                start = pl.multiple_of(jnp.clip(g * tq - half_w, 0, L - W), HEAD_DIM)
                k = k_ref[pl.ds(start, W), :]
                v = v_ref[pl.ds(start, W), :]
            q = q_ref[rows, :]
            variant = var_ref[g]
            for pair in range(2):
                sl = slice(pair * LANES, (pair + 1) * LANES)
                qp, kp, vp = q[:, sl], k[:, sl], v[:, sl]
                outs, lses = [], []
                for hh in range(2):
                    head = 2 * pair + hh
                    mine = low if hh == 0 else jnp.logical_not(low)
                    qm = jnp.where(mine, qp, jnp.zeros_like(qp))
                    s = _dot_nt(qm, kp) + bias_ref[variant, head]
                    sk = sink_ref[head] * LOG2E if has_sink else None
                    if exact:
                        m = jnp.max(s, axis=-1, keepdims=True)
                        if has_sink:
                            m = jnp.maximum(m, sk)
                        s = s - m
                        sk = sk - m if has_sink else None
                    p = jnp.exp2(s).astype(BF16)
                    vx = jnp.where(low_w, vp, one) if hh == 0 else jnp.where(low_w, one, vp)
                    acc = _dot(p, vx)
                    if has_sink:
                        acc = acc + jnp.where(mine, 0.0, jnp.exp2(sk))
                    den = pltpu.roll(acc, HEAD_DIM, 1)
                    outs.append(acc / den)
                    if want_lse:
                        lses.append(jnp.log(den) + m * LN2 if exact else jnp.log(den))
                    if not exact:
                        good = jnp.isfinite(acc) & (jnp.where(mine, den, acc) >= 2.0 ** -64)
                        bad = jnp.maximum(bad, jnp.max(jnp.where(good, 0.0, 1.0)))
                o_ref[rows, sl] = jnp.where(low, outs[0], outs[1]).astype(BF16)
                if want_lse:
                    lse_ref[rows, sl] = jnp.where(low, lses[0], lses[1])
        return bad

    bad = run(exact=False)

    @pl.when(bad != 0.0)
    def _():
        run(exact=True)


def _band_bias(tq, L, W, half_w, slopes):
    offs = [int(np.clip(i * tq - half_w, 0, L - W)) - i * tq for i in range(L // tq)]
    uniq = sorted(set(offs))
    var = np.asarray([uniq.index(o) for o in offs], np.int32)
    rel = (np.arange(W)[None, :] - np.arange(tq)[:, None])[None] + np.asarray(uniq)[:, None, None]
    dist = np.abs(rel).astype(np.float64)
    sl = np.asarray(slopes, np.float64)[None, :, None, None]
    bias = np.where(dist[:, None] <= half_w, -sl * dist[:, None] * LOG2E, NEG_INF)
    return jnp.asarray(var), jnp.asarray(bias, F32)


def _band_attention(qkv, half_w, slopes, sink=None, want_lse=True, tq=256):
    N, L, _ = qkv.shape
    tq = min(tq, L)
    W = min(L, tq + 2 * half_w)
    nq = 4 * HEAD_DIM
    var, bias = _band_bias(tq, L, W, half_w, slopes)
    nsub = max(1, min(4, L // tq))
    ts = tq * nsub
    in_specs = [
        pl.BlockSpec((None, ts, nq), lambda n, i, var: (n, i, 0)),
        pl.BlockSpec((None, L, nq), lambda n, i, var: (n, 0, 1)),
        pl.BlockSpec((None, L, nq), lambda n, i, var: (n, 0, 2)),
        pl.BlockSpec(bias.shape, lambda n, i, var: (0, 0, 0, 0), pipeline_mode=pl.Buffered(1)),
    ]
    args = [qkv, qkv, qkv, bias]
    if sink is not None:
        in_specs = [pl.BlockSpec(memory_space=pltpu.SMEM)] + in_specs
        args = [sink] + args
    out_shape = [jax.ShapeDtypeStruct((N, L, nq), BF16)]
    out_specs = [pl.BlockSpec((None, ts, nq), lambda n, i, var: (n, i, 0))]
    if want_lse:
        out_shape.append(jax.ShapeDtypeStruct((N, L, nq), F32))
        out_specs.append(pl.BlockSpec((None, ts, nq), lambda n, i, var: (n, i, 0)))
    outs = pl.pallas_call(
        functools.partial(_band_kernel, tq=tq, nsub=nsub, L=L, W=W, half_w=half_w,
                          has_sink=sink is not None, want_lse=want_lse),
        grid_spec=pltpu.PrefetchScalarGridSpec(
            num_scalar_prefetch=1, grid=(N, L // ts), in_specs=in_specs, out_specs=out_specs),
        out_shape=out_shape,
        compiler_params=pltpu.CompilerParams(
            dimension_semantics=("parallel", "parallel"), vmem_limit_bytes=VMEM_LIMIT),
        name="band_attention",
    )(var, *args)
    return outs


def _out_proj_kernel(x_ref, ya_ref, o1_ref, l1_ref, o2_ref, l2_ref, o3_ref, l3_ref, oc_ref,
                     wo_ref, g2_ref, wr_ref, cnt0_ref, xm_ref, hg_ref, meta_ref, cnt_ref,
                     scr_ref, cnt_scr, *, tm):
    nq = DIL_HEADS * HEAD_DIM
    ncs = nq // LANES
    slab = 0
    merged = []
    for r, o_ref, l_ref in ((DIL_PAIRS[1][1], o2_ref, l2_ref), (DIL_PAIRS[2][1], o3_ref, l3_ref)):
        for j in range(r):
            oj = o_ref[j].astype(F32)
            lj = l_ref[j]
            for c in range(ncs):
                scr_ref[slab + c, pl.ds(j, tm // r, stride=r), :] = oj[:, c * LANES:(c + 1) * LANES]
                scr_ref[slab + ncs + c, pl.ds(j, tm // r, stride=r), :] = lj[:, c * LANES:(c + 1) * LANES]
        on = jnp.concatenate([scr_ref[slab + c] for c in range(ncs)], axis=1)
        ln = jnp.concatenate([scr_ref[slab + ncs + c] for c in range(ncs)], axis=1)
        merged.append((on, ln))
        slab += 2 * ncs
    o1 = o1_ref[...].astype(F32)
    l1 = l1_ref[...]
    (o2, l2), (o3, l3) = merged
    mx = jnp.maximum(l1, jnp.maximum(l2, l3))
    e1, e2, e3 = jnp.exp(l1 - mx), jnp.exp(l2 - mx), jnp.exp(l3 - mx)
    yb = (e1 * o1 + e2 * o2 + e3 * o3) / (e1 + e2 + e3)

    na = MLA_HEADS * MLA_V
    y = (_dot(ya_ref[...], wo_ref[0:na, :])
         + _dot(yb.astype(BF16), wo_ref[na:na + nq, :])
         + _dot(oc_ref[...], wo_ref[na + nq:, :]))
    xm = x_ref[...] + y
    xm_ref[...] = xm
    h2f = _rms(xm, g2_ref[...])
    h2 = h2f.astype(BF16)

    logits = _dot(h2, wr_ref[...])
    lane = lax.broadcasted_iota(jnp.int32, (tm, LANES), 1).astype(F32)
    big = float(LANES)
    lg = jnp.where(lane < MOE_GROUPS, logits, NEG_INF)
    gmax = jnp.max(lg, axis=-1, keepdims=True)
    gsel = jnp.min(jnp.where(lg == gmax, lane, big), axis=-1, keepdims=True)
    gden = jnp.sum(jnp.where(lane < MOE_GROUPS, jnp.exp(lg - gmax), 0.0), axis=-1, keepdims=True)
    gprob = 1.0 / gden
    lo = MOE_GROUPS + MOE_EPG * gsel
    le = jnp.where((lane >= lo) & (lane < lo + MOE_EPG), logits, NEG_INF)
    t1 = jnp.max(le, axis=-1, keepdims=True)
    i1 = jnp.min(jnp.where(le == t1, lane, big), axis=-1, keepdims=True)
    le2 = jnp.where(lane == i1, NEG_INF, le)
    t2 = jnp.max(le2, axis=-1, keepdims=True)
    i2 = jnp.min(jnp.where(le2 == t2, lane, big), axis=-1, keepdims=True)
    ex = jnp.exp(t2 - t1)
    gate1 = gprob / (1.0 + ex)
    gate2 = gprob * ex / (1.0 + ex)
    e1, e2 = i1 - MOE_GROUPS, i2 - MOE_GROUPS
    first = e1 < e2
    la = jnp.minimum(e1, e2) - MOE_EPG * gsel
    lb = jnp.maximum(e1, e2) - MOE_EPG * gsel
    cls = gsel * PAIRS_PER_GROUP + la * (2 * MOE_EPG - 1 - la) * 0.5 + (lb - la - 1.0)
    onehot = lane == cls

    @pl.when((pl.program_id(0) == 0) & (pl.program_id(1) == 0))
    def _():
        cnt_scr[...] = cnt0_ref[...]

    tri = (lax.broadcasted_iota(jnp.int32, (tm, tm), 0) >= lax.broadcasted_iota(jnp.int32, (tm, tm), 1))
    prefix = _dot(tri.astype(BF16), onehot.astype(BF16))
    base = cnt_scr[...]
    rank = jnp.sum(jnp.where(onehot, prefix + base, 0.0), axis=-1, keepdims=True) - 1.0
    cnt_scr[...] = base + prefix[tm - 1:tm, :]
    cnt_ref[...] = cnt_scr[...]
    meta_ref[...] = jnp.where(lane == 0, cls, jnp.where(lane == 1, rank, 0.0))

    for c in range(D_MODEL // LANES):
        hg_ref[pl.ds(c, tm, stride=ROW_CHUNKS), :] = h2f[:, c * LANES:(c + 1) * LANES]
    hg_ref[pl.ds(ROW_CHUNKS - 2, tm, stride=ROW_CHUNKS), :] = jnp.broadcast_to(
        jnp.where(first, gate1, gate2), (tm, LANES))
    hg_ref[pl.ds(ROW_CHUNKS - 1, tm, stride=ROW_CHUNKS), :] = jnp.broadcast_to(
        jnp.where(first, gate2, gate1), (tm, LANES))


def _out_proj(x, ya, b1, b2, b3, oc, cnt0, lw, tm=2 * TOKEN_TILE):
    B, S, _ = x.shape
    nq = DIL_HEADS * HEAD_DIM
    r2, r3 = DIL_PAIRS[1][1], DIL_PAIRS[2][1]
    tok = lambda c: pl.BlockSpec((None, tm, c), lambda b, i: (b, i, 0))
    res = lambda r: pl.BlockSpec((None, r, tm // r, nq), lambda b, i: (b, 0, i, 0))
    cnt_spec = pl.BlockSpec((1, LANES), lambda b, i: (0, 0))
    return pl.pallas_call(
        functools.partial(_out_proj_kernel, tm=tm),
        grid=(B, S // tm),
        in_specs=[tok(D_MODEL), tok(MLA_HEADS * MLA_V), tok(nq), tok(nq), res(r2), res(r2),
                  res(r3), res(r3), tok(nq),
                  _const_spec((D_MODEL, D_MODEL)), _const_spec((1, D_MODEL)),
                  _const_spec((D_MODEL, LANES)), cnt_spec],
        out_specs=[tok(D_MODEL),
                   pl.BlockSpec((tm * ROW_CHUNKS, LANES), lambda b, i: (b * (S // tm) + i, 0)),
                   tok(LANES), cnt_spec],
        out_shape=[jax.ShapeDtypeStruct((B, S, D_MODEL), F32),
                   jax.ShapeDtypeStruct((B * S * ROW_CHUNKS, LANES), F32),
                   jax.ShapeDtypeStruct((B, S, LANES), F32),
                   jax.ShapeDtypeStruct((1, LANES), F32)],
        scratch_shapes=[pltpu.VMEM((4 * (nq // LANES), tm, LANES), F32), pltpu.VMEM((1, LANES), F32)],
        compiler_params=pltpu.CompilerParams(
            dimension_semantics=("arbitrary", "arbitrary"), vmem_limit_bytes=VMEM_LIMIT),
        name="out_proj_router",
    )(x, ya, b1[0], b1[1], b2[0].reshape(B, r2, S // r2, nq), b2[1].reshape(B, r2, S // r2, nq),
      b3[0].reshape(B, r3, S // r3, nq), b3[1].reshape(B, r3, S // r3, nq), oc,
      lw["w_out"], lw["norm2_g"], lw["w_router"], cnt0)


def _row_copy(src_ref, src_row, dst_ref, dst_row, chunks, sem):
    return pltpu.make_async_copy(src_ref.at[pl.ds(src_row * chunks, chunks)],
                                 dst_ref.at[pl.ds(dst_row * chunks, chunks)], sem)


def _dispatch_kernel(pos_ref, h_ref, init_ref, out_ref, sem, *, chunks):
    del init_ref
    n = h_ref.shape[0] // chunks

    def start(i, c):
        for prio in range(2):
            r = 2 * i + prio
            _row_copy(h_ref, r, out_ref, pos_ref[0, r], chunks, sem).start(priority=prio)
        return c

    lax.fori_loop(0, n // 2, start, 0, unroll=4)
    pltpu.make_async_copy(h_ref, out_ref.at[pl.ds(0, n * chunks)], sem).wait()


def _dispatch_rows(hg, pos, buf, rows=ROW_STEP, chunks=ROW_CHUNKS):
    T = pos.shape[0]
    return pl.pallas_call(
        functools.partial(_dispatch_kernel, chunks=chunks),
        grid=(T // rows,),
        in_specs=[pl.BlockSpec((None, 1, rows), lambda i: (i, 0, 0), memory_space=pltpu.SMEM),
                  pl.BlockSpec((rows * chunks, LANES), lambda i: (i, 0)),
                  pl.BlockSpec(memory_space=pl.ANY)],
        out_specs=pl.BlockSpec(memory_space=pl.ANY),
        out_shape=jax.ShapeDtypeStruct(buf.shape, buf.dtype),
        scratch_shapes=[pltpu.SemaphoreType.DMA(())],
        input_output_aliases={2: 0},
        compiler_params=pltpu.CompilerParams(
            dimension_semantics=("arbitrary",), vmem_limit_bytes=VMEM_LIMIT),
        name="moe_dispatch",
    )(pos.reshape(T // rows, 1, rows), hg, buf)


def _gather_kernel(pos_ref, y_ref, out_ref, sem, *, chunks):
    n = out_ref.shape[0] // chunks

    def start(i, c):
        for prio in range(2):
            r = 2 * i + prio
            _row_copy(y_ref, pos_ref[0, r], out_ref, r, chunks, sem).start(priority=prio)
        return c

    lax.fori_loop(0, n // 2, start, 0, unroll=4)
    pltpu.make_async_copy(y_ref.at[pl.ds(0, n * chunks)], out_ref, sem).wait()


def _gather_rows(y, pos, rows=ROW_STEP, chunks=OUT_CHUNKS):
    T = pos.shape[0]
    return pl.pallas_call(
        functools.partial(_gather_kernel, chunks=chunks),
        grid=(T // rows,),
        in_specs=[pl.BlockSpec((None, 1, rows), lambda i: (i, 0, 0), memory_space=pltpu.SMEM),
                  pl.BlockSpec(memory_space=pl.ANY)],
        out_specs=pl.BlockSpec((rows * chunks, LANES), lambda i: (i, 0)),
        out_shape=jax.ShapeDtypeStruct((T * chunks, LANES), y.dtype),
        scratch_shapes=[pltpu.SemaphoreType.DMA(())],
        compiler_params=pltpu.CompilerParams(
            dimension_semantics=("arbitrary",), vmem_limit_bytes=VMEM_LIMIT),
        name="moe_gather",
    )(pos.reshape(T // rows, 1, rows), y)


def _load_rows(ref, n, chunks, first=0, count=None):
    count = chunks - first if count is None else count
    return jnp.concatenate([ref[pl.ds(first + c, n, stride=chunks), :] for c in range(count)], axis=1)


def _moe_kernel(ea_ref, eb_ref, valid_ref, hs_ref, wgu_a, wd_a, wgu_b, wd_b, y_ref, *, tile):
    i = pl.program_id(0)

    @pl.when(valid_ref[i] == 1)
    def _():
        nh = D_MODEL // LANES
        h = _load_rows(hs_ref, tile, ROW_CHUNKS, 0, nh).astype(BF16)

        def ffn(wgu_ref, wd_ref, g):
            au = _dot(h, wgu_ref[...])
            a, u = au[:, :MOE_HIDDEN], au[:, MOE_HIDDEN:]
            z = (a * jax.nn.sigmoid(a) * u) * jnp.concatenate([g, g], axis=1)
            return _dot(z.astype(BF16), wd_ref[...])

        y = (ffn(wgu_a, wd_a, hs_ref[pl.ds(nh, tile, stride=ROW_CHUNKS), :])
             + ffn(wgu_b, wd_b, hs_ref[pl.ds(nh + 1, tile, stride=ROW_CHUNKS), :]))
        for c in range(OUT_CHUNKS):
            y_ref[pl.ds(c, tile, stride=OUT_CHUNKS), :] = y[:, c * LANES:(c + 1) * LANES]

    @pl.when(valid_ref[i] == 0)
    def _():
        y_ref[...] = jnp.zeros_like(y_ref)


def _moe_experts(hs, ea, eb, valid, lw, tile=MOE_TILE):
    P = hs.shape[0] // ROW_CHUNKS
    row = lambda c: pl.BlockSpec((tile * c, LANES), lambda i, ea, eb, va: (i, 0))
    wgu = lambda which: pl.BlockSpec(
        (None, D_MODEL, 2 * MOE_HIDDEN),
        (lambda i, ea, eb, va: (ea[i], 0, 0)) if which == 0 else (lambda i, ea, eb, va: (eb[i], 0, 0)))
    wd = lambda which: pl.BlockSpec(
        (None, MOE_HIDDEN, D_MODEL),
        (lambda i, ea, eb, va: (ea[i], 0, 0)) if which == 0 else (lambda i, ea, eb, va: (eb[i], 0, 0)))
    return pl.pallas_call(
        functools.partial(_moe_kernel, tile=tile),
        grid_spec=pltpu.PrefetchScalarGridSpec(
            num_scalar_prefetch=3,
            grid=(P // tile,),
            in_specs=[row(ROW_CHUNKS), wgu(0), wd(0), wgu(1), wd(1)],
            out_specs=row(OUT_CHUNKS),
        ),
        out_shape=jax.ShapeDtypeStruct((P * OUT_CHUNKS, LANES), F32),
        compiler_params=pltpu.CompilerParams(
            dimension_semantics=("arbitrary",), vmem_limit_bytes=VMEM_LIMIT),
        name="moe_experts",
    )(ea, eb, valid, hs, lw["w_gu"], lw["w_d"], lw["w_gu"], lw["w_d"])


def _final_kernel(x_ref, y_ref, g_ref, o_ref, *, tm):
    o_ref[...] = _rms(x_ref[...] + _load_rows(y_ref, tm, OUT_CHUNKS), g_ref[...])


def _final_norm(x, y, g, tm=512):
    B, S, _ = x.shape
    tok = pl.BlockSpec((None, tm, D_MODEL), lambda b, i: (b, i, 0))
    yrow = pl.BlockSpec((tm * OUT_CHUNKS, LANES), lambda b, i: (b * (S // tm) + i, 0))
    return pl.pallas_call(
        functools.partial(_final_kernel, tm=tm),
        grid=(B, S // tm),
        in_specs=[tok, yrow, _const_spec((1, D_MODEL))],
        out_specs=tok,
        out_shape=jax.ShapeDtypeStruct((B, S, D_MODEL), F32),
        compiler_params=pltpu.CompilerParams(dimension_semantics=("parallel", "parallel")),
        name="final_norm",
    )(x, y, g)


def _prep_layer(p, l):
    w_in = p["w_in"][l]
    dm = w_in.shape[0]
    z = lambda n, rows=dm: jnp.zeros((rows, n), F32)
    kr = w_in[:, OFF_KRA:A_COLS]
    pc = w_in[:, A_COLS + B_COLS:]
    nq, nk = SWA_Q_HEADS * HEAD_DIM, SWA_KV_HEADS * HEAD_DIM
    dup = lambda w: jnp.concatenate([w[:, :HEAD_DIM]] * 2 + [w[:, HEAD_DIM:]] * 2, axis=1)
    w_big = jnp.concatenate([
        w_in[:, :OFF_KRA],
        z(MLA_NOPE), kr, z(MLA_HEAD_PAD - MLA_NOPE - MLA_ROPE),
        w_in[:, A_COLS:A_COLS + B_COLS],
        pc[:, :nq], dup(pc[:, nq:nq + nk]), dup(pc[:, nq + nk:]),
    ], axis=1).astype(BF16)

    w_uq = p["mla_w_uq"][l].reshape(MLA_Q_LORA, MLA_HEADS, MLA_NOPE + MLA_ROPE)
    pad = MLA_HEAD_PAD - MLA_NOPE - MLA_ROPE
    zq = lambda n: jnp.zeros((MLA_Q_LORA, MLA_HEADS, n), F32)
    w_q = jnp.concatenate([w_uq, zq(pad)], axis=2).reshape(MLA_Q_LORA, -1).astype(BF16)

    w_ukv = p["mla_w_ukv"][l].reshape(MLA_KV_LORA, MLA_HEADS, MLA_NOPE + MLA_V)
    zk = jnp.zeros((MLA_KV_LORA, MLA_HEADS, MLA_HEAD_PAD - MLA_NOPE), F32)
    wk = jnp.concatenate([w_ukv[:, :, :MLA_NOPE], zk], axis=2)
    wv = jnp.concatenate([w_ukv[:, :, MLA_NOPE:], zk], axis=2)
    w_kv = jnp.concatenate([wk.reshape(MLA_KV_LORA, -1), wv.reshape(MLA_KV_LORA, -1)], axis=1).astype(BF16)

    w_router = jnp.concatenate([p["w_router_group"][l], p["w_router_expert"][l],
                                z(LANES - MOE_GROUPS - MOE_EXPERTS)], axis=1).astype(BF16)
    return {
        "norm1_g": p["norm1_g"][l][None, :], "w_big": w_big,
        "q_norm_g": p["mla_q_norm_g"][l][None, :], "w_q": w_q,
        "kv_norm_g": p["mla_kv_norm_g"][l][None, :], "w_kv": w_kv,
        "sink": p["swa_sink"][l].astype(F32),
        "w_out": p["w_out"][l].astype(BF16), "norm2_g": p["norm2_g"][l][None, :],
        "w_router": w_router,
        "w_gu": jnp.concatenate([p["w_gate"][l], p["w_up"][l]], axis=2).astype(BF16),
        "w_d": p["w_down"][l].astype(BF16),
    }


def _rope_tables(S):
    pos = jnp.arange(S, dtype=F32)
    freqs = ROPE_THETA ** (-jnp.arange(0, MLA_ROPE, 2, dtype=F32) / MLA_ROPE)
    ang = pos[:, None] * freqs[None, :]
    cos, sin = jnp.cos(ang), jnp.sin(ang)
    pad = MLA_HEAD_PAD - MLA_NOPE - MLA_ROPE
    cos_t = jnp.concatenate([jnp.ones((S, MLA_NOPE), F32), cos, cos, jnp.zeros((S, pad), F32)], axis=1)
    sin_t = jnp.concatenate([jnp.zeros((S, MLA_NOPE), F32), sin, sin, jnp.zeros((S, pad), F32)], axis=1)
    return cos_t, sin_t


def _pair_tables():
    ea, eb = [], []
    for g in range(MOE_GROUPS):
        for a in range(MOE_EPG):
            for b in range(a + 1, MOE_EPG):
                ea.append(g * MOE_EPG + a)
                eb.append(g * MOE_EPG + b)
    return np.asarray(ea, np.int32), np.asarray(eb, np.int32)


def _moe(hgs, metas, counts, lw, buf, tile=MOE_TILE):
    T = sum(m.shape[0] * m.shape[1] for m in metas)
    P = T + N_PAIR_CLASSES * tile
    cnt = counts[0, :N_PAIR_CLASSES].astype(jnp.int32)
    padded = ((cnt + tile - 1) // tile) * tile
    ends = jnp.cumsum(padded)
    starts = ends - padded
    tile_start = jnp.arange(P // tile, dtype=jnp.int32) * tile
    tcls = jnp.sum((tile_start[:, None] >= ends[None, :]).astype(jnp.int32), axis=1)
    valid = (tile_start < ends[-1]).astype(jnp.int32)
    tcls = jnp.where(valid == 1, tcls, jnp.max(jnp.where(valid == 1, tcls, 0)))
    ta, tb = _pair_tables()
    ea, eb = jnp.asarray(ta)[tcls], jnp.asarray(tb)[tcls]

    poss = []
    if buf is None:
        buf = jnp.zeros((P * ROW_CHUNKS, LANES), F32)
    classes = jnp.arange(N_PAIR_CLASSES, dtype=jnp.int32)
    for hg, meta in zip(hgs, metas):
        m = meta.reshape(-1, LANES)
        cls = m[:, 0].astype(jnp.int32)
        first = jnp.sum(jnp.where(cls[:, None] == classes[None, :], starts[None, :], 0), axis=1)
        pos = first + m[:, 1].astype(jnp.int32)
        poss.append(pos)
        buf = _dispatch_rows(hg, pos, buf)
    y = _moe_experts(buf, ea, eb, valid, lw)
    return [_gather_rows(y, pos) for pos in poss], buf


def _mixers(x, y, cnt0, lw, tabs, slopes):
    B, S, _ = x.shape
    x, (q, k, v, g1, g2, g3, gc) = _in_proj(x, y, lw, tabs)
    ya = _mla_attention(q, k, v)
    nb = B_GROUP_COLS
    sl_c, sl_b = slopes[:SWA_Q_HEADS], slopes[SWA_Q_HEADS:].reshape(3, DIL_HEADS)
    outs = []
    for gi, arr in enumerate((g1, g2, g3)):
        window, r = DIL_PAIRS[gi]
        outs.append(_band_attention(arr.reshape(B * r, S // r, nb), window // (2 * r), sl_b[gi] * r))
    oc = _band_attention(gc, SWA_HALF_WINDOW, sl_c, sink=lw["sink"], want_lse=False)[0]
    return _out_proj(x, ya, outs[0], outs[1], outs[2], oc, cnt0, lw)


def kernel(x_prompt, x_sample, norm1_g, w_in, mla_q_norm_g, mla_w_uq, mla_kv_norm_g, mla_w_ukv, swa_sink,
           w_out, norm2_g, w_router_group, w_router_expert, w_gate, w_up, w_down, final_norm_g):
    p = dict(norm1_g=norm1_g, w_in=w_in, mla_q_norm_g=mla_q_norm_g, mla_w_uq=mla_w_uq,
             mla_kv_norm_g=mla_kv_norm_g, mla_w_ukv=mla_w_ukv, swa_sink=swa_sink, w_out=w_out,
             norm2_g=norm2_g, w_router_group=w_router_group, w_router_expert=w_router_expert,
             w_gate=w_gate, w_up=w_up, w_down=w_down)
    slopes = _alibi_slopes()
    xs = [x_prompt, x_sample]
    tabs = [_rope_tables(x.shape[1]) for x in xs]
    ys = [None, None]
    buf = None
    for l in range(DEPTH):
        lw = _prep_layer(p, l)
        xm, hg, meta = [], [], []
        cnt = jnp.zeros((1, LANES), F32)
        for i in range(2):
            a, b, c, cnt = _mixers(xs[i], ys[i], cnt, lw, tabs[i], slopes)
            xm.append(a)
            hg.append(b)
            meta.append(c)
        xs = xm
        ys, buf = _moe(hg, meta, cnt, lw, buf)
    g = final_norm_g[None, :]
    return tuple(_final_norm(xs[i], ys[i], g) for i in range(2))
```

```python
import functools

import numpy as np
import jax
import jax.numpy as jnp
from jax import lax
from jax.experimental import pallas as pl
from jax.experimental.pallas import tpu as pltpu

F32 = jnp.float32
BF16 = jnp.bfloat16

D_MODEL = 1024
DEPTH = 2
HEAD_DIM = 64
MLA_HEADS = 8
MLA_Q_LORA = 384
MLA_KV_LORA = 256
MLA_NOPE = 64
MLA_ROPE = 32
MLA_V = 64
ROPE_THETA = 10000.0
DIL_PAIRS = ((128, 1), (512, 4), (2048, 16))
DIL_HEADS = 4
SWA_Q_HEADS = 4
SWA_KV_HEADS = 2
SWA_HALF_WINDOW = 128
N_ALIBI_HEADS = 16
MOE_GROUPS = 4
MOE_EPG = 8
MOE_EXPERTS = 32
MOE_HIDDEN = 256
RMS_EPS = 1e-6
NEG_INF = -1e30

LANES = 128
MLA_HEAD_PAD = 128
A_COLS = MLA_Q_LORA + MLA_KV_LORA + MLA_ROPE
B_GROUP_COLS = 3 * DIL_HEADS * HEAD_DIM
B_COLS = 3 * B_GROUP_COLS
OFF_CQ = 0
OFF_CKV = MLA_Q_LORA
OFF_KRA = OFF_CKV + MLA_KV_LORA
OFF_G1 = OFF_KRA + LANES
OFF_G2 = OFF_G1 + B_GROUP_COLS
OFF_G3 = OFF_G2 + B_GROUP_COLS
OFF_GC = OFF_G3 + B_GROUP_COLS
BIG_COLS = OFF_GC + B_GROUP_COLS

LOG2E = float(np.log2(np.e))
LN2 = float(np.log(2.0))
BAND_Q_SCALE = float(HEAD_DIM ** -0.5 * np.log2(np.e))
MLA_Q_SCALE = float((MLA_NOPE + MLA_ROPE) ** -0.5 * np.log2(np.e))
PAIRS_PER_GROUP = MOE_EPG * (MOE_EPG - 1) // 2
N_PAIR_CLASSES = MOE_GROUPS * PAIRS_PER_GROUP
ROW_CHUNKS = D_MODEL // LANES + 2
OUT_CHUNKS = D_MODEL // LANES
ROW_STEP = 1024
MOE_TILE = 512
TOKEN_TILE = 256
VMEM_LIMIT = 56 * 1024 * 1024


def _alibi_slopes():
    return 2.0 ** (-8.0 * np.arange(1, N_ALIBI_HEADS + 1, dtype=np.float64) / N_ALIBI_HEADS)


def _rms(x, g):
    return x * lax.rsqrt(jnp.mean(x * x, axis=-1, keepdims=True) + RMS_EPS) * g


def _dot(a, b):
    return jnp.dot(a, b, preferred_element_type=F32)


def _dot_nt(a, b):
    return lax.dot_general(a, b, (((1,), (1,)), ((), ())), preferred_element_type=F32)


def _const_spec(shape):
    nd = len(shape)
    return pl.BlockSpec(shape, lambda *_: (0,) * nd, pipeline_mode=pl.Buffered(1))


def _in_proj_kernel(*refs, has_y, tm):
    if has_y:
        x_ref, y_ref = refs[:2]
        refs = refs[2:]
    else:
        x_ref = refs[0]
        refs = refs[1:]
    (g1_ref, wbig_ref, qg_ref, wq_ref, kvg_ref, wkv_ref, cos_ref, sin_ref) = refs[:8]
    refs = refs[8:]
    if has_y:
        xo_ref = refs[0]
        refs = refs[1:]
    q_ref, k_ref, v_ref, b1_ref, b2_ref, b3_ref, c_ref, scr_ref = refs

    x = x_ref[...]
    if has_y:
        x = x + _load_rows(y_ref, tm, OUT_CHUNKS)
        xo_ref[...] = x
    h = _rms(x, g1_ref[...]).astype(BF16)
    proj = _dot(h, wbig_ref[...])

    cos = cos_ref[...]
    sin = sin_ref[...]
    cos8 = jnp.concatenate([cos] * MLA_HEADS, axis=1)
    sin8 = jnp.concatenate([sin] * MLA_HEADS, axis=1)
    hw = MLA_HEADS * MLA_HEAD_PAD

    hl1 = lax.broadcasted_iota(jnp.int32, (tm, MLA_HEAD_PAD), 1)
    first_half = hl1 < MLA_NOPE + MLA_ROPE // 2

    def rot(b):
        half = MLA_ROPE // 2
        return jnp.where(first_half, -pltpu.roll(b, MLA_HEAD_PAD - half, 1), pltpu.roll(b, half, 1))

    cqn = _rms(proj[:, OFF_CQ:OFF_CKV], qg_ref[...]).astype(BF16)
    qa = _dot(cqn, wq_ref[...])
    qr = jnp.concatenate([rot(qa[:, h * MLA_HEAD_PAD:(h + 1) * MLA_HEAD_PAD]) for h in range(MLA_HEADS)],
                         axis=1)
    q = (qa * cos8 + qr * sin8) * MLA_Q_SCALE
    q_ref[...] = q.astype(BF16)

    ckvn = _rms(proj[:, OFF_CKV:OFF_KRA], kvg_ref[...]).astype(BF16)
    kva = _dot(ckvn, wkv_ref[...])
    kr = proj[:, OFF_KRA:OFF_G1]
    kpe = kr * cos + rot(kr) * sin
    hl = lax.broadcasted_iota(jnp.int32, (tm, hw), 1) % MLA_HEAD_PAD
    k = kva[:, :hw] + jnp.concatenate([kpe] * MLA_HEADS, axis=1) + (hl == MLA_HEAD_PAD - 1).astype(F32)
    k_ref[...] = k.astype(BF16)
    v_ref[...] = (kva[:, hw:] + (hl >= MLA_V).astype(F32)).astype(BF16)

    nslab = B_GROUP_COLS // LANES
    nq_slab = DIL_HEADS * HEAD_DIM // LANES

    def slab(off, c):
        s = proj[:, off + c * LANES: off + (c + 1) * LANES]
        return s * BAND_Q_SCALE if c < nq_slab else s

    b1_ref[...] = jnp.concatenate([slab(OFF_G1, c) for c in range(nslab)], axis=1).astype(BF16)
    c_ref[...] = jnp.concatenate([slab(OFF_GC, c) for c in range(nslab)], axis=1).astype(BF16)

    for off, r, out_ref in ((OFF_G2, DIL_PAIRS[1][1], b2_ref), (OFF_G3, DIL_PAIRS[2][1], b3_ref)):
        for c in range(nslab):
            scr_ref[c] = slab(off, c)
        for j in range(r):
            rows = [scr_ref[c, pl.ds(j, tm // r, stride=r), :] for c in range(nslab)]
            out_ref[j] = jnp.concatenate(rows, axis=1).astype(BF16)


def _in_proj(x, y, lw, tabs, tm=TOKEN_TILE):
    B, S, _ = x.shape
    has_y = y is not None
    hw = MLA_HEADS * MLA_HEAD_PAD
    r2, r3 = DIL_PAIRS[1][1], DIL_PAIRS[2][1]
    tok = lambda c: pl.BlockSpec((None, tm, c), lambda b, i: (b, i, 0))
    in_specs = [tok(D_MODEL)]
    args = [x]
    if has_y:
        in_specs.append(pl.BlockSpec((tm * OUT_CHUNKS, LANES), lambda b, i: (b * (S // tm) + i, 0)))
        args.append(y)
    in_specs += [
        _const_spec((1, D_MODEL)), _const_spec((D_MODEL, BIG_COLS)),
        _const_spec((1, MLA_Q_LORA)), _const_spec((MLA_Q_LORA, hw)),
        _const_spec((1, MLA_KV_LORA)), _const_spec((MLA_KV_LORA, 2 * hw)),
        pl.BlockSpec((tm, LANES), lambda b, i: (i, 0)),
        pl.BlockSpec((tm, LANES), lambda b, i: (i, 0)),
    ]
    args += [lw["norm1_g"], lw["w_big"], lw["q_norm_g"], lw["w_q"], lw["kv_norm_g"], lw["w_kv"],
             tabs[0], tabs[1]]
    out_shape, out_specs = [], []
    if has_y:
        out_shape.append(jax.ShapeDtypeStruct((B, S, D_MODEL), F32))
        out_specs.append(tok(D_MODEL))
    out_shape += [
        jax.ShapeDtypeStruct((B, S, hw), BF16), jax.ShapeDtypeStruct((B, S, hw), BF16),
        jax.ShapeDtypeStruct((B, S, hw), BF16),
        jax.ShapeDtypeStruct((B, S, B_GROUP_COLS), BF16),
        jax.ShapeDtypeStruct((B, r2, S // r2, B_GROUP_COLS), BF16),
        jax.ShapeDtypeStruct((B, r3, S // r3, B_GROUP_COLS), BF16),
        jax.ShapeDtypeStruct((B, S, B_GROUP_COLS), BF16),
    ]
    out_specs += [
        tok(hw), tok(hw), tok(hw), tok(B_GROUP_COLS),
        pl.BlockSpec((None, r2, tm // r2, B_GROUP_COLS), lambda b, i: (b, 0, i, 0)),
        pl.BlockSpec((None, r3, tm // r3, B_GROUP_COLS), lambda b, i: (b, 0, i, 0)),
        tok(B_GROUP_COLS),
    ]
    outs = pl.pallas_call(
        functools.partial(_in_proj_kernel, has_y=has_y, tm=tm),
        grid=(B, S // tm),
        in_specs=in_specs,
        out_specs=out_specs,
        out_shape=out_shape,
        scratch_shapes=[pltpu.VMEM((B_GROUP_COLS // LANES, tm, LANES), F32)],
        compiler_params=pltpu.CompilerParams(
            dimension_semantics=("parallel", "parallel"), vmem_limit_bytes=VMEM_LIMIT),
        name="in_proj",
    )(*args)
    if has_y:
        return outs[0], outs[1:]
    return x, outs


def _mla_kernel(q_ref, k_ref, v_ref, o_ref, *, tq, tk, nk):
    hp = MLA_HEAD_PAD
    lane = lax.broadcasted_iota(jnp.int32, (tq, LANES), 1)
    low = lane < MLA_V
    qs = [q_ref[:, h * hp:(h + 1) * hp] for h in range(2)]

    def finish(accs):
        o0 = accs[0] / pltpu.roll(accs[0], MLA_V, 1)
        o1 = pltpu.roll(accs[1], MLA_V, 1) / accs[1]
        o_ref[...] = jnp.where(low, o0, o1).astype(BF16)

    qx = []
    for h in range(2):
        s0 = _dot_nt(qs[h], k_ref[0:LANES, h * hp:(h + 1) * hp])
        shift = jnp.max(s0, axis=-1, keepdims=True).astype(BF16)
        qx.append(jnp.where(lane == hp - 1, -shift, qs[h]))
    accs = [jnp.zeros((tq, LANES), F32) for _ in range(2)]
    for j in range(nk):
        for h in range(2):
            ks = k_ref[j * tk:(j + 1) * tk, h * hp:(h + 1) * hp]
            vs = v_ref[j * tk:(j + 1) * tk, h * hp:(h + 1) * hp]
            p = jnp.exp2(_dot_nt(qx[h], ks)).astype(BF16)
            accs[h] = accs[h] + _dot(p, vs)
    bad = jnp.max(jnp.where(jnp.isfinite(accs[0]) & jnp.isfinite(accs[1]), 0.0, 1.0))
    finish(accs)

    @pl.when(bad != 0.0)
    def _():
        def body(j, carry):
            start = pl.multiple_of(j * tk, tk)
            new = []
            for h in range(2):
                m, acc = carry[h]
                ks = k_ref[pl.ds(start, tk), h * hp:(h + 1) * hp]
                vs = v_ref[pl.ds(start, tk), h * hp:(h + 1) * hp]
                s = _dot_nt(qs[h], ks)
                mn = jnp.maximum(m, jnp.max(s, axis=-1, keepdims=True))
                p = jnp.exp2(s - mn).astype(BF16)
                new.append((mn, jnp.exp2(m - mn) * acc + _dot(p, vs)))
            return tuple(new)

        init = tuple((jnp.full((tq, 1), NEG_INF, F32), jnp.zeros((tq, LANES), F32)) for _ in range(2))
        res = lax.fori_loop(0, nk, body, init)
        finish([res[0][1], res[1][1]])


def _mla_attention(q, k, v, tq=1024, tk=512):
    B, S, _ = q.shape
    tk = min(tk, S)
    return pl.pallas_call(
        functools.partial(_mla_kernel, tq=tq, tk=tk, nk=S // tk),
        grid=(B, MLA_HEADS // 2, S // tq),
        in_specs=[
            pl.BlockSpec((None, tq, 2 * MLA_HEAD_PAD), lambda b, h, i: (b, i, h)),
            pl.BlockSpec((None, S, 2 * MLA_HEAD_PAD), lambda b, h, i: (b, 0, h)),
            pl.BlockSpec((None, S, 2 * MLA_HEAD_PAD), lambda b, h, i: (b, 0, h)),
        ],
        out_specs=pl.BlockSpec((None, tq, 2 * MLA_V), lambda b, h, i: (b, i, h)),
        out_shape=jax.ShapeDtypeStruct((B, S, MLA_HEADS * MLA_V), BF16),
        compiler_params=pltpu.CompilerParams(
            dimension_semantics=("parallel", "parallel", "parallel"), vmem_limit_bytes=VMEM_LIMIT),
        name="mla_attention",
    )(q, k, v)


def _band_kernel(*refs, tq, nsub, L, W, half_w, packed, has_sink, want_lse):
    var_ref = refs[0]
    refs = refs[1:]
    if has_sink:
        sink_ref = refs[0]
        refs = refs[1:]
    q_ref, k_ref, v_ref, bias_ref = refs[:4]
    o_ref = refs[4]
    lse_ref = refs[5] if want_lse else None

    lane = lax.broadcasted_iota(jnp.int32, (tq, LANES), 1)
    low = lane < HEAD_DIM
    lane_w = lax.broadcasted_iota(jnp.int32, (W, LANES), 1)
    low_w = lane_w < HEAD_DIM
    one = jnp.ones((), BF16)

    def run(exact):
        bad = jnp.zeros((), F32)
        for sub in range(nsub):
            g = pl.program_id(1) * nsub + sub
            rows = slice(sub * tq, (sub + 1) * tq)
            if W == L:
                k = k_ref[...]
                v = v_ref[...]
            else:
                start = g * tq if packed else jnp.clip(g * tq - half_w, 0, L - W)
                start = pl.multiple_of(start, HEAD_DIM)
                k = k_ref[pl.ds(start, W), :]
                v = v_ref[pl.ds(start, W), :]
            q = q_ref[rows, :]
            variant = var_ref[g]
            for pair in range(2):
                sl = slice(pair * LANES, (pair + 1) * LANES)
                qp, kp, vp = q[:, sl], k[:, sl], v[:, sl]
                outs, lses = [], []
                for hh in range(2):
                    head = 2 * pair + hh
                    mine = low if hh == 0 else jnp.logical_not(low)
                    qm = jnp.where(mine, qp, jnp.zeros_like(qp))
                    s = _dot_nt(qm, kp) + bias_ref[variant, head]
                    sk = sink_ref[head] * LOG2E if has_sink else None
                    if exact:
                        m = jnp.max(s, axis=-1, keepdims=True)
                        if has_sink:
                            m = jnp.maximum(m, sk)
                        s = s - m
                        sk = sk - m if has_sink else None
                    p = jnp.exp2(s).astype(BF16)
                    vx = jnp.where(low_w, vp, one) if hh == 0 else jnp.where(low_w, one, vp)
                    acc = _dot(p, vx)
                    if has_sink:
                        acc = acc + jnp.where(mine, 0.0, jnp.exp2(sk))
                    den = pltpu.roll(acc, HEAD_DIM, 1)
                    outs.append(acc / den)
                    if want_lse:
                        lses.append(jnp.log(den) + m * LN2 if exact else jnp.log(den))
                    if not exact:
                        good = jnp.isfinite(acc) & (jnp.where(mine, den, acc) >= 2.0 ** -64)
                        bad = jnp.maximum(bad, jnp.max(jnp.where(good, 0.0, 1.0)))
                o_ref[rows, sl] = jnp.where(low, outs[0], outs[1]).astype(BF16)
                if want_lse:
                    lse_ref[rows, sl] = jnp.where(low, lses[0], lses[1])
        return bad

    bad = run(exact=False)

    @pl.when(bad != 0.0)
    def _():
        run(exact=True)


def _band_bias(tq, L, W, half_w, slopes):
    offs = [int(np.clip(i * tq - half_w, 0, L - W)) - i * tq for i in range(L // tq)]
    uniq = sorted(set(offs))
    var = np.asarray([uniq.index(o) for o in offs], np.int32)
    rel = (np.arange(W)[None, :] - np.arange(tq)[:, None])[None] + np.asarray(uniq)[:, None, None]
    dist = np.abs(rel).astype(np.float64)
    sl = np.asarray(slopes, np.float64)[None, :, None, None]
    bias = np.where(dist[:, None] <= half_w, -sl * dist[:, None] * LOG2E, NEG_INF)
    return jnp.asarray(var), jnp.asarray(bias, F32)


def _band_attention(qkv, half_w, slopes, sink=None, want_lse=True, tq=256):
    N, L, _ = qkv.shape
    tq = min(tq, L)
    W = min(L, tq + 2 * half_w)
    nq = 4 * HEAD_DIM
    var, bias = _band_bias(tq, L, W, half_w, slopes)
    pack = 4 if (L == tq and N % 4 == 0) else 1
    packed = pack > 1
    if packed:
        qkv = qkv.reshape(N // pack, pack * L, qkv.shape[2])
        var = jnp.tile(var, pack)
        N, L = N // pack, pack * L
    nsub = max(1, min(4, L // tq))
    ts = tq * nsub
    in_specs = [
        pl.BlockSpec((None, ts, nq), lambda n, i, var: (n, i, 0)),
        pl.BlockSpec((None, L, nq), lambda n, i, var: (n, 0, 1)),
        pl.BlockSpec((None, L, nq), lambda n, i, var: (n, 0, 2)),
        pl.BlockSpec(bias.shape, lambda n, i, var: (0, 0, 0, 0), pipeline_mode=pl.Buffered(1)),
    ]
    args = [qkv, qkv, qkv, bias]
    if sink is not None:
        in_specs = [pl.BlockSpec(memory_space=pltpu.SMEM)] + in_specs
        args = [sink] + args
    out_shape = [jax.ShapeDtypeStruct((N, L, nq), BF16)]
    out_specs = [pl.BlockSpec((None, ts, nq), lambda n, i, var: (n, i, 0))]
    if want_lse:
        out_shape.append(jax.ShapeDtypeStruct((N, L, nq), F32))
        out_specs.append(pl.BlockSpec((None, ts, nq), lambda n, i, var: (n, i, 0)))
    outs = pl.pallas_call(
        functools.partial(_band_kernel, tq=tq, nsub=nsub, L=L, W=W, half_w=half_w, packed=packed,
                          has_sink=sink is not None, want_lse=want_lse),
        grid_spec=pltpu.PrefetchScalarGridSpec(
            num_scalar_prefetch=1, grid=(N, L // ts), in_specs=in_specs, out_specs=out_specs),
        out_shape=out_shape,
        compiler_params=pltpu.CompilerParams(
            dimension_semantics=("parallel", "parallel"), vmem_limit_bytes=VMEM_LIMIT),
        name="band_attention",
    )(var, *args)
    if packed:
        outs = [o.reshape(N * pack, L // pack, nq) for o in outs]
    return outs


def _out_proj_kernel(x_ref, ya_ref, o1_ref, l1_ref, o2_ref, l2_ref, o3_ref, l3_ref, oc_ref,
                     wo_ref, g2_ref, wr_ref, cnt0_ref, xm_ref, hg_ref, meta_ref, cnt_ref,
                     scr_ref, cnt_scr, *, tm):
    nq = DIL_HEADS * HEAD_DIM
    ncs = nq // LANES
    slab = 0
    merged = []
    for r, o_ref, l_ref in ((DIL_PAIRS[1][1], o2_ref, l2_ref), (DIL_PAIRS[2][1], o3_ref, l3_ref)):
        for j in range(r):
            oj = o_ref[j].astype(F32)
            lj = l_ref[j]
            for c in range(ncs):
                scr_ref[slab + c, pl.ds(j, tm // r, stride=r), :] = oj[:, c * LANES:(c + 1) * LANES]
                scr_ref[slab + ncs + c, pl.ds(j, tm // r, stride=r), :] = lj[:, c * LANES:(c + 1) * LANES]
        on = jnp.concatenate([scr_ref[slab + c] for c in range(ncs)], axis=1)
        ln = jnp.concatenate([scr_ref[slab + ncs + c] for c in range(ncs)], axis=1)
        merged.append((on, ln))
        slab += 2 * ncs
    o1 = o1_ref[...].astype(F32)
    l1 = l1_ref[...]
    (o2, l2), (o3, l3) = merged
    mx = jnp.maximum(l1, jnp.maximum(l2, l3))
    e1, e2, e3 = jnp.exp(l1 - mx), jnp.exp(l2 - mx), jnp.exp(l3 - mx)
    yb = (e1 * o1 + e2 * o2 + e3 * o3) / (e1 + e2 + e3)

    na = MLA_HEADS * MLA_V
    y = (_dot(ya_ref[...], wo_ref[0:na, :])
         + _dot(yb.astype(BF16), wo_ref[na:na + nq, :])
         + _dot(oc_ref[...], wo_ref[na + nq:, :]))
    xm = x_ref[...] + y
    xm_ref[...] = xm
    h2f = _rms(xm, g2_ref[...])
    h2 = h2f.astype(BF16)

    logits = _dot(h2, wr_ref[...])
    lane = lax.broadcasted_iota(jnp.int32, (tm, LANES), 1).astype(F32)
    big = float(LANES)
    lg = jnp.where(lane < MOE_GROUPS, logits, NEG_INF)
    gmax = jnp.max(lg, axis=-1, keepdims=True)
    gsel = jnp.min(jnp.where(lg == gmax, lane, big), axis=-1, keepdims=True)
    gden = jnp.sum(jnp.where(lane < MOE_GROUPS, jnp.exp(lg - gmax), 0.0), axis=-1, keepdims=True)
    gprob = 1.0 / gden
    lo = MOE_GROUPS + MOE_EPG * gsel
    le = jnp.where((lane >= lo) & (lane < lo + MOE_EPG), logits, NEG_INF)
    t1 = jnp.max(le, axis=-1, keepdims=True)
    i1 = jnp.min(jnp.where(le == t1, lane, big), axis=-1, keepdims=True)
    le2 = jnp.where(lane == i1, NEG_INF, le)
    t2 = jnp.max(le2, axis=-1, keepdims=True)
    i2 = jnp.min(jnp.where(le2 == t2, lane, big), axis=-1, keepdims=True)
    ex = jnp.exp(t2 - t1)
    gate1 = gprob / (1.0 + ex)
    gate2 = gprob * ex / (1.0 + ex)
    e1, e2 = i1 - MOE_GROUPS, i2 - MOE_GROUPS
    first = e1 < e2
    la = jnp.minimum(e1, e2) - MOE_EPG * gsel
    lb = jnp.maximum(e1, e2) - MOE_EPG * gsel
    cls = gsel * PAIRS_PER_GROUP + la * (2 * MOE_EPG - 1 - la) * 0.5 + (lb - la - 1.0)
    onehot = lane == cls

    @pl.when((pl.program_id(0) == 0) & (pl.program_id(1) == 0))
    def _():
        cnt_scr[...] = cnt0_ref[...]

    tri = (lax.broadcasted_iota(jnp.int32, (tm, tm), 0) >= lax.broadcasted_iota(jnp.int32, (tm, tm), 1))
    prefix = _dot(tri.astype(BF16), onehot.astype(BF16))
    base = cnt_scr[...]
    rank = jnp.sum(jnp.where(onehot, prefix + base, 0.0), axis=-1, keepdims=True) - 1.0
    cnt_scr[...] = base + prefix[tm - 1:tm, :]
    cnt_ref[...] = cnt_scr[...]
    meta_ref[...] = jnp.where(lane == 0, cls, jnp.where(lane == 1, rank, 0.0))

    for c in range(D_MODEL // LANES):
        hg_ref[pl.ds(c, tm, stride=ROW_CHUNKS), :] = h2f[:, c * LANES:(c + 1) * LANES]
    hg_ref[pl.ds(ROW_CHUNKS - 2, tm, stride=ROW_CHUNKS), :] = jnp.broadcast_to(
        jnp.where(first, gate1, gate2), (tm, LANES))
    hg_ref[pl.ds(ROW_CHUNKS - 1, tm, stride=ROW_CHUNKS), :] = jnp.broadcast_to(
        jnp.where(first, gate2, gate1), (tm, LANES))


def _out_proj(x, ya, b1, b2, b3, oc, cnt0, lw, tm=2 * TOKEN_TILE):
    B, S, _ = x.shape
    nq = DIL_HEADS * HEAD_DIM
    r2, r3 = DIL_PAIRS[1][1], DIL_PAIRS[2][1]
    tok = lambda c: pl.BlockSpec((None, tm, c), lambda b, i: (b, i, 0))
    res = lambda r: pl.BlockSpec((None, r, tm // r, nq), lambda b, i: (b, 0, i, 0))
    cnt_spec = pl.BlockSpec((1, LANES), lambda b, i: (0, 0))
    return pl.pallas_call(
        functools.partial(_out_proj_kernel, tm=tm),
        grid=(B, S // tm),
        in_specs=[tok(D_MODEL), tok(MLA_HEADS * MLA_V), tok(nq), tok(nq), res(r2), res(r2),
                  res(r3), res(r3), tok(nq),
                  _const_spec((D_MODEL, D_MODEL)), _const_spec((1, D_MODEL)),
                  _const_spec((D_MODEL, LANES)), cnt_spec],
        out_specs=[tok(D_MODEL),
                   pl.BlockSpec((tm * ROW_CHUNKS, LANES), lambda b, i: (b * (S // tm) + i, 0)),
                   tok(LANES), cnt_spec],
        out_shape=[jax.ShapeDtypeStruct((B, S, D_MODEL), F32),
                   jax.ShapeDtypeStruct((B * S * ROW_CHUNKS, LANES), F32),
                   jax.ShapeDtypeStruct((B, S, LANES), F32),
                   jax.ShapeDtypeStruct((1, LANES), F32)],
        scratch_shapes=[pltpu.VMEM((4 * (nq // LANES), tm, LANES), F32), pltpu.VMEM((1, LANES), F32)],
        compiler_params=pltpu.CompilerParams(
            dimension_semantics=("arbitrary", "arbitrary"), vmem_limit_bytes=VMEM_LIMIT),
        name="out_proj_router",
    )(x, ya, b1[0], b1[1], b2[0].reshape(B, r2, S // r2, nq), b2[1].reshape(B, r2, S // r2, nq),
      b3[0].reshape(B, r3, S // r3, nq), b3[1].reshape(B, r3, S // r3, nq), oc,
      lw["w_out"], lw["norm2_g"], lw["w_router"], cnt0)


def _row_copy(src_ref, src_row, dst_ref, dst_row, chunks, sem):
    return pltpu.make_async_copy(src_ref.at[pl.ds(src_row * chunks, chunks)],
                                 dst_ref.at[pl.ds(dst_row * chunks, chunks)], sem)


def _dispatch_kernel(pos_ref, h_ref, init_ref, out_ref, sem, *, chunks):
    del init_ref
    n = h_ref.shape[0] // chunks

    def start(i, c):
        for prio in range(2):
            r = 2 * i + prio
            _row_copy(h_ref, r, out_ref, pos_ref[0, r], chunks, sem).start(priority=prio)
        return c

    lax.fori_loop(0, n // 2, start, 0, unroll=4)
    pltpu.make_async_copy(h_ref, out_ref.at[pl.ds(0, n * chunks)], sem).wait()


def _dispatch_rows(hg, pos, buf, rows=ROW_STEP, chunks=ROW_CHUNKS):
    T = pos.shape[0]
    return pl.pallas_call(
        functools.partial(_dispatch_kernel, chunks=chunks),
        grid=(T // rows,),
        in_specs=[pl.BlockSpec((None, 1, rows), lambda i: (i, 0, 0), memory_space=pltpu.SMEM),
                  pl.BlockSpec((rows * chunks, LANES), lambda i: (i, 0)),
                  pl.BlockSpec(memory_space=pl.ANY)],
        out_specs=pl.BlockSpec(memory_space=pl.ANY),
        out_shape=jax.ShapeDtypeStruct(buf.shape, buf.dtype),
        scratch_shapes=[pltpu.SemaphoreType.DMA(())],
        input_output_aliases={2: 0},
        compiler_params=pltpu.CompilerParams(
            dimension_semantics=("arbitrary",), vmem_limit_bytes=VMEM_LIMIT),
        name="moe_dispatch",
    )(pos.reshape(T // rows, 1, rows), hg, buf)


def _gather_kernel(pos_ref, y_ref, out_ref, sem, *, chunks):
    n = out_ref.shape[0] // chunks

    def start(i, c):
        for prio in range(2):
            r = 2 * i + prio
            _row_copy(y_ref, pos_ref[0, r], out_ref, r, chunks, sem).start(priority=prio)
        return c

    lax.fori_loop(0, n // 2, start, 0, unroll=4)
    pltpu.make_async_copy(y_ref.at[pl.ds(0, n * chunks)], out_ref, sem).wait()


def _gather_rows(y, pos, rows=ROW_STEP, chunks=OUT_CHUNKS):
    T = pos.shape[0]
    return pl.pallas_call(
        functools.partial(_gather_kernel, chunks=chunks),
        grid=(T // rows,),
        in_specs=[pl.BlockSpec((None, 1, rows), lambda i: (i, 0, 0), memory_space=pltpu.SMEM),
                  pl.BlockSpec(memory_space=pl.ANY)],
        out_specs=pl.BlockSpec((rows * chunks, LANES), lambda i: (i, 0)),
        out_shape=jax.ShapeDtypeStruct((T * chunks, LANES), y.dtype),
        scratch_shapes=[pltpu.SemaphoreType.DMA(())],
        compiler_params=pltpu.CompilerParams(
            dimension_semantics=("arbitrary",), vmem_limit_bytes=VMEM_LIMIT),
        name="moe_gather",
    )(pos.reshape(T // rows, 1, rows), y)


def _load_rows(ref, n, chunks, first=0, count=None):
    count = chunks - first if count is None else count
    return jnp.concatenate([ref[pl.ds(first + c, n, stride=chunks), :] for c in range(count)], axis=1)


def _moe_kernel(ea_ref, eb_ref, valid_ref, hs_ref, wgu_a, wd_a, wgu_b, wd_b, y_ref, *, tile):
    i = pl.program_id(0)

    @pl.when(valid_ref[i] == 1)
    def _():
        nh = D_MODEL // LANES
        h = _load_rows(hs_ref, tile, ROW_CHUNKS, 0, nh).astype(BF16)

        def ffn(wgu_ref, wd_ref, g):
            au = _dot(h, wgu_ref[...])
            a, u = au[:, :MOE_HIDDEN], au[:, MOE_HIDDEN:]
            z = (a * jax.nn.sigmoid(a) * u) * jnp.concatenate([g, g], axis=1)
            return _dot(z.astype(BF16), wd_ref[...])

        y = (ffn(wgu_a, wd_a, hs_ref[pl.ds(nh, tile, stride=ROW_CHUNKS), :])
             + ffn(wgu_b, wd_b, hs_ref[pl.ds(nh + 1, tile, stride=ROW_CHUNKS), :]))
        for c in range(OUT_CHUNKS):
            y_ref[pl.ds(c, tile, stride=OUT_CHUNKS), :] = y[:, c * LANES:(c + 1) * LANES]

    @pl.when(valid_ref[i] == 0)
    def _():
        y_ref[...] = jnp.zeros_like(y_ref)


def _moe_experts(hs, ea, eb, valid, lw, tile=MOE_TILE):
    P = hs.shape[0] // ROW_CHUNKS
    row = lambda c: pl.BlockSpec((tile * c, LANES), lambda i, ea, eb, va: (i, 0))
    wgu = lambda which: pl.BlockSpec(
        (None, D_MODEL, 2 * MOE_HIDDEN),
        (lambda i, ea, eb, va: (ea[i], 0, 0)) if which == 0 else (lambda i, ea, eb, va: (eb[i], 0, 0)))
    wd = lambda which: pl.BlockSpec(
        (None, MOE_HIDDEN, D_MODEL),
        (lambda i, ea, eb, va: (ea[i], 0, 0)) if which == 0 else (lambda i, ea, eb, va: (eb[i], 0, 0)))
    return pl.pallas_call(
        functools.partial(_moe_kernel, tile=tile),
        grid_spec=pltpu.PrefetchScalarGridSpec(
            num_scalar_prefetch=3,
            grid=(P // tile,),
            in_specs=[row(ROW_CHUNKS), wgu(0), wd(0), wgu(1), wd(1)],
            out_specs=row(OUT_CHUNKS),
        ),
        out_shape=jax.ShapeDtypeStruct((P * OUT_CHUNKS, LANES), F32),
        compiler_params=pltpu.CompilerParams(
            dimension_semantics=("arbitrary",), vmem_limit_bytes=VMEM_LIMIT),
        name="moe_experts",
    )(ea, eb, valid, hs, lw["w_gu"], lw["w_d"], lw["w_gu"], lw["w_d"])


def _final_kernel(x_ref, y_ref, g_ref, o_ref, *, tm):
    o_ref[...] = _rms(x_ref[...] + _load_rows(y_ref, tm, OUT_CHUNKS), g_ref[...])


def _final_norm(x, y, g, tm=512):
    B, S, _ = x.shape
    tok = pl.BlockSpec((None, tm, D_MODEL), lambda b, i: (b, i, 0))
    yrow = pl.BlockSpec((tm * OUT_CHUNKS, LANES), lambda b, i: (b * (S // tm) + i, 0))
    return pl.pallas_call(
        functools.partial(_final_kernel, tm=tm),
        grid=(B, S // tm),
        in_specs=[tok, yrow, _const_spec((1, D_MODEL))],
        out_specs=tok,
        out_shape=jax.ShapeDtypeStruct((B, S, D_MODEL), F32),
        compiler_params=pltpu.CompilerParams(dimension_semantics=("parallel", "parallel")),
        name="final_norm",
    )(x, y, g)


def _prep_layer(p, l):
    w_in = p["w_in"][l]
    dm = w_in.shape[0]
    z = lambda n, rows=dm: jnp.zeros((rows, n), F32)
    kr = w_in[:, OFF_KRA:A_COLS]
    pc = w_in[:, A_COLS + B_COLS:]
    nq, nk = SWA_Q_HEADS * HEAD_DIM, SWA_KV_HEADS * HEAD_DIM
    dup = lambda w: jnp.concatenate([w[:, :HEAD_DIM]] * 2 + [w[:, HEAD_DIM:]] * 2, axis=1)
    w_big = jnp.concatenate([
        w_in[:, :OFF_KRA],
        z(MLA_NOPE), kr, z(MLA_HEAD_PAD - MLA_NOPE - MLA_ROPE),
        w_in[:, A_COLS:A_COLS + B_COLS],
        pc[:, :nq], dup(pc[:, nq:nq + nk]), dup(pc[:, nq + nk:]),
    ], axis=1).astype(BF16)

    w_uq = p["mla_w_uq"][l].reshape(MLA_Q_LORA, MLA_HEADS, MLA_NOPE + MLA_ROPE)
    pad = MLA_HEAD_PAD - MLA_NOPE - MLA_ROPE
    zq = lambda n: jnp.zeros((MLA_Q_LORA, MLA_HEADS, n), F32)
    w_q = jnp.concatenate([w_uq, zq(pad)], axis=2).reshape(MLA_Q_LORA, -1).astype(BF16)

    w_ukv = p["mla_w_ukv"][l].reshape(MLA_KV_LORA, MLA_HEADS, MLA_NOPE + MLA_V)
    zk = jnp.zeros((MLA_KV_LORA, MLA_HEADS, MLA_HEAD_PAD - MLA_NOPE), F32)
    wk = jnp.concatenate([w_ukv[:, :, :MLA_NOPE], zk], axis=2)
    wv = jnp.concatenate([w_ukv[:, :, MLA_NOPE:], zk], axis=2)
    w_kv = jnp.concatenate([wk.reshape(MLA_KV_LORA, -1), wv.reshape(MLA_KV_LORA, -1)], axis=1).astype(BF16)

    w_router = jnp.concatenate([p["w_router_group"][l], p["w_router_expert"][l],
                                z(LANES - MOE_GROUPS - MOE_EXPERTS)], axis=1).astype(BF16)
    return {
        "norm1_g": p["norm1_g"][l][None, :], "w_big": w_big,
        "q_norm_g": p["mla_q_norm_g"][l][None, :], "w_q": w_q,
        "kv_norm_g": p["mla_kv_norm_g"][l][None, :], "w_kv": w_kv,
        "sink": p["swa_sink"][l].astype(F32),
        "w_out": p["w_out"][l].astype(BF16), "norm2_g": p["norm2_g"][l][None, :],
        "w_router": w_router,
        "w_gu": jnp.concatenate([p["w_gate"][l], p["w_up"][l]], axis=2).astype(BF16),
        "w_d": p["w_down"][l].astype(BF16),
    }


def _rope_tables(S):
    pos = jnp.arange(S, dtype=F32)
    freqs = ROPE_THETA ** (-jnp.arange(0, MLA_ROPE, 2, dtype=F32) / MLA_ROPE)
    ang = pos[:, None] * freqs[None, :]
    cos, sin = jnp.cos(ang), jnp.sin(ang)
    pad = MLA_HEAD_PAD - MLA_NOPE - MLA_ROPE
    cos_t = jnp.concatenate([jnp.ones((S, MLA_NOPE), F32), cos, cos, jnp.zeros((S, pad), F32)], axis=1)
    sin_t = jnp.concatenate([jnp.zeros((S, MLA_NOPE), F32), sin, sin, jnp.zeros((S, pad), F32)], axis=1)
    return cos_t, sin_t


def _pair_tables():
    ea, eb = [], []
    for g in range(MOE_GROUPS):
        for a in range(MOE_EPG):
            for b in range(a + 1, MOE_EPG):
                ea.append(g * MOE_EPG + a)
                eb.append(g * MOE_EPG + b)
    return np.asarray(ea, np.int32), np.asarray(eb, np.int32)


def _moe(hgs, metas, counts, lw, buf, tile=MOE_TILE):
    T = sum(m.shape[0] * m.shape[1] for m in metas)
    P = T + N_PAIR_CLASSES * tile
    cnt = counts[0, :N_PAIR_CLASSES].astype(jnp.int32)
    padded = ((cnt + tile - 1) // tile) * tile
    ends = jnp.cumsum(padded)
    starts = ends - padded
    tile_start = jnp.arange(P // tile, dtype=jnp.int32) * tile
    tcls = jnp.sum((tile_start[:, None] >= ends[None, :]).astype(jnp.int32), axis=1)
    valid = (tile_start < ends[-1]).astype(jnp.int32)
    tcls = jnp.where(valid == 1, tcls, jnp.max(jnp.where(valid == 1, tcls, 0)))
    ta, tb = _pair_tables()
    ea, eb = jnp.asarray(ta)[tcls], jnp.asarray(tb)[tcls]

    poss = []
    if buf is None:
        buf = jnp.zeros((P * ROW_CHUNKS, LANES), F32)
    classes = jnp.arange(N_PAIR_CLASSES, dtype=jnp.int32)
    for hg, meta in zip(hgs, metas):
        m = meta.reshape(-1, LANES)
        cls = m[:, 0].astype(jnp.int32)
        first = jnp.sum(jnp.where(cls[:, None] == classes[None, :], starts[None, :], 0), axis=1)
        pos = first + m[:, 1].astype(jnp.int32)
        poss.append(pos)
        buf = _dispatch_rows(hg, pos, buf)
    y = _moe_experts(buf, ea, eb, valid, lw)
    return [_gather_rows(y, pos) for pos in poss], buf


def _mixers(x, y, cnt0, lw, tabs, slopes):
    B, S, _ = x.shape
    x, (q, k, v, g1, g2, g3, gc) = _in_proj(x, y, lw, tabs)
    ya = _mla_attention(q, k, v)
    nb = B_GROUP_COLS
    sl_c, sl_b = slopes[:SWA_Q_HEADS], slopes[SWA_Q_HEADS:].reshape(3, DIL_HEADS)
    outs = []
    for gi, arr in enumerate((g1, g2, g3)):
        window, r = DIL_PAIRS[gi]
        outs.append(_band_attention(arr.reshape(B * r, S // r, nb), window // (2 * r), sl_b[gi] * r))
    oc = _band_attention(gc, SWA_HALF_WINDOW, sl_c, sink=lw["sink"], want_lse=False)[0]
    return _out_proj(x, ya, outs[0], outs[1], outs[2], oc, cnt0, lw)


def kernel(x_prompt, x_sample, norm1_g, w_in, mla_q_norm_g, mla_w_uq, mla_kv_norm_g, mla_w_ukv, swa_sink,
           w_out, norm2_g, w_router_group, w_router_expert, w_gate, w_up, w_down, final_norm_g):
    p = dict(norm1_g=norm1_g, w_in=w_in, mla_q_norm_g=mla_q_norm_g, mla_w_uq=mla_w_uq,
             mla_kv_norm_g=mla_kv_norm_g, mla_w_ukv=mla_w_ukv, swa_sink=swa_sink, w_out=w_out,
             norm2_g=norm2_g, w_router_group=w_router_group, w_router_expert=w_router_expert,
             w_gate=w_gate, w_up=w_up, w_down=w_down)
    slopes = _alibi_slopes()
    xs = [x_prompt, x_sample]
    tabs = [_rope_tables(x.shape[1]) for x in xs]
    ys = [None, None]
    buf = None
    for l in range(DEPTH):
        lw = _prep_layer(p, l)
        xm, hg, meta = [], [], []
        cnt = jnp.zeros((1, LANES), F32)
        for i in range(2):
            a, b, c, cnt = _mixers(xs[i], ys[i], cnt, lw, tabs[i], slopes)
            xm.append(a)
            hg.append(b)
            meta.append(c)
        xs = xm
        ys, buf = _moe(hg, meta, cnt, lw, buf)
    g = final_norm_g[None, :]
    return tuple(_final_norm(xs[i], ys[i], g) for i in range(2))
```
